```python
import jax, jax.numpy as jnp
from jax import lax
import numpy as np


D_MODEL = 1024
BATCH = 4
SEQ = 4096
DEPTH = 4
DEC_BATCH = 32
DEC_SEQ = 1
PAST_LEN = 8192
PAGE_SIZE = 128

H_A = 8
DH_A = 64
W_A = H_A * DH_A
MOBA_BLOCK = 256
MOBA_TOPK = 3
Q_CHUNK = 64
CONV_W = 3
W_B = 512
H_C = 4
DK_C = 64
DV_C = 128
QK_C = H_C * DK_C
V_C = H_C * DV_C
FG_RANK = 16
GATE_TEMP = 16.0
GLA_CHUNK = 16
N_BRANCH = 3
N_GROUPS = 4
EXPERTS_PER_GROUP = 4
N_EXPERTS = N_GROUPS * EXPERTS_PER_GROUP
D_EXPERT = 512
TOP_E = 2
ALPHA = (2 * DEPTH) ** 0.25
BETA = (8 * DEPTH) ** -0.25
LN_EPS = 1e-5
NEG_INF = -1e30
IN_SPLITS = (W_A, W_A, W_A, W_B, W_B, W_B, QK_C, QK_C, V_C, V_C, FG_RANK, N_BRANCH * D_MODEL)
IN_TOTAL = 3 * W_A + 3 * W_B + 2 * QK_C + 2 * V_C + FG_RANK + N_BRANCH * D_MODEL

kernel_name = 'hybrid_moba_conv_gla_hmoe_step'


def layer_norm(x, g, b):
    xf = x.astype(jnp.float32)
    mu = xf.mean(-1, keepdims=True)
    var = jnp.square(xf - mu).mean(-1, keepdims=True)
    return ((xf - mu) * lax.rsqrt(var + LN_EPS) * g + b).astype(x.dtype)


def moba_attention(q, k_all, v_all):
    bsz, t_len, n_h, dh = q.shape
    l_len = k_all.shape[1]
    q0 = l_len - t_len
    qc_len = Q_CHUNK if t_len % Q_CHUNK == 0 else t_len
    n_chunks = t_len // qc_len
    nbk = max(l_len // MOBA_BLOCK, MOBA_TOPK)
    region = MOBA_BLOCK + qc_len
    lp = nbk * MOBA_BLOCK + region
    pad = ((0, 0), (0, lp - l_len), (0, 0), (0, 0))
    k_pad = jnp.pad(k_all, pad)
    v_pad = jnp.pad(v_all, pad)
    k_blocks = k_pad[:, :nbk * MOBA_BLOCK].reshape(bsz, nbk, MOBA_BLOCK, n_h, dh)
    v_blocks = v_pad[:, :nbk * MOBA_BLOCK].reshape(bsz, nbk, MOBA_BLOCK, n_h, dh)
    k_mean = k_blocks.astype(jnp.float32).mean(axis=2)
    b_ix = jnp.arange(bsz)[:, None, None, None]
    h_ix = jnp.arange(n_h)[None, :, None, None]
    scale = dh ** -0.5
    n_sel = MOBA_TOPK * MOBA_BLOCK

    def one_chunk(args):
        qc, start = args
        qpos = start + jnp.arange(qc_len)
        qblk = qpos // MOBA_BLOCK
        sel = jnp.einsum('bqhd,bnhd->bhqn', qc.astype(jnp.float32), k_mean)
        fully_past = jnp.arange(nbk)[None, :] < qblk[:, None]
        sel = jnp.where(fully_past[None, None], sel, NEG_INF)
        _, idx = lax.top_k(sel, MOBA_TOPK)
        valid = jnp.arange(MOBA_TOPK)[None, :] < jnp.minimum(qblk, MOBA_TOPK)[:, None]
        kg = k_blocks[b_ix, idx, :, h_ix]
        vg = v_blocks[b_ix, idx, :, h_ix]
        s_sel = jnp.einsum('bqhd,bhqrkd->bhqrk', qc, kg).astype(jnp.float32) * scale
        s_sel = jnp.where(valid[None, None, :, :, None], s_sel, NEG_INF)
        r0 = (start // MOBA_BLOCK) * MOBA_BLOCK
        ko = lax.dynamic_slice_in_dim(k_pad, r0, region, axis=1)
        vo = lax.dynamic_slice_in_dim(v_pad, r0, region, axis=1)
        kpos = r0 + jnp.arange(region)
        own = (kpos[None, :] // MOBA_BLOCK == qblk[:, None]) & (kpos[None, :] <= qpos[:, None])
        s_own = jnp.einsum('bqhd,brhd->bhqr', qc, ko).astype(jnp.float32) * scale
        s_own = jnp.where(own[None, None], s_own, NEG_INF)
        s_all = jnp.concatenate([s_sel.reshape(bsz, n_h, qc_len, n_sel), s_own], axis=-1)
        p = jax.nn.softmax(s_all, axis=-1).astype(v_all.dtype)
        p_sel = p[..., :n_sel].reshape(bsz, n_h, qc_len, MOBA_TOPK, MOBA_BLOCK)
        return (jnp.einsum('bhqrk,bhqrkd->bqhd', p_sel, vg)
                + jnp.einsum('bhqr,brhd->bqhd', p[..., n_sel:], vo))

    qs = q.reshape(bsz, n_chunks, qc_len, n_h, dh).swapaxes(0, 1)
    starts = q0 + jnp.arange(n_chunks, dtype=jnp.int32) * qc_len
    out = lax.map(one_chunk, (qs, starts))
    return out.swapaxes(0, 1).reshape(bsz, t_len, n_h, dh)


def gla_recurrence(q, k, v, log_a, s0):
    bsz, t_len = q.shape[:2]
    c = GLA_CHUNK if t_len % GLA_CHUNK == 0 else t_len
    nc = t_len // c

    def to_chunks(a):
        return a.reshape(bsz, nc, c, *a.shape[2:]).swapaxes(0, 1)

    causal = jnp.tril(jnp.ones((c, c), dtype=bool))

    def step(s, inp):
        qc, kc, vc, lac = inp
        b = jnp.cumsum(lac, axis=1)
        o_inter = jnp.einsum('bthk,bhkv->bthv', qc * jnp.exp(b), s)
        diff = b[:, :, None] - b[:, None, :]
        decay = jnp.exp(jnp.where(causal[None, :, :, None, None], diff, -jnp.inf))
        att = jnp.einsum('bthk,bshk,btshk->bhts', qc, kc, decay)
        o_intra = jnp.einsum('bhts,bshv->bthv', att, vc)
        b_last = b[:, -1]
        s_new = (jnp.exp(b_last)[..., None] * s
                 + jnp.einsum('bshk,bshv->bhkv', kc * jnp.exp(b_last[:, None] - b), vc))
        return s_new, o_inter + o_intra

    s_fin, o = lax.scan(step, s0, (to_chunks(q), to_chunks(k), to_chunks(v), to_chunks(log_a)))
    return o.swapaxes(0, 1).reshape(bsz, t_len, *o.shape[3:]), s_fin


def token_mixers(x, k_past, v_past, conv_prev, gla_prev, w_in, conv_w, w_fg2, b_fg, g_gla,
                 w_br_a, w_br_b, w_br_c, w_out):
    bsz, t_len, _ = x.shape
    z = x @ w_in
    split_points = np.cumsum(IN_SPLITS)[:-1].tolist()
    (q_a, k_a, v_a, b_gate, c_gate, h_b, q_c, k_c, v_c, r_c, f_low, gate_logits) = jnp.split(z, split_points, axis=-1)
    k_a = k_a.reshape(bsz, t_len, H_A, DH_A)
    v_a = v_a.reshape(bsz, t_len, H_A, DH_A)
    k_all = jnp.concatenate([k_past.astype(k_a.dtype), k_a], axis=1)
    v_all = jnp.concatenate([v_past.astype(v_a.dtype), v_a], axis=1)
    y_a = moba_attention(q_a.reshape(bsz, t_len, H_A, DH_A), k_all, v_all).reshape(bsz, t_len, W_A)
    u = c_gate * h_b
    u_pad = jnp.concatenate([conv_prev.astype(u.dtype), u], axis=1)
    conv = conv_w[0] * u_pad[:, 0:t_len]
    for j in range(1, CONV_W):
        conv = conv + conv_w[j] * u_pad[:, j:j + t_len]
    y_b = b_gate * conv
    conv_state = u_pad[:, t_len:]
    log_a = jax.nn.log_sigmoid((f_low @ w_fg2 + b_fg).astype(jnp.float32)) / GATE_TEMP
    qg = q_c.reshape(bsz, t_len, H_C, DK_C).astype(jnp.float32) * (DK_C ** -0.5)
    kg = k_c.reshape(bsz, t_len, H_C, DK_C).astype(jnp.float32)
    vg = v_c.reshape(bsz, t_len, H_C, DV_C).astype(jnp.float32)
    o, s_fin = gla_recurrence(qg, kg, vg, log_a.reshape(bsz, t_len, H_C, DK_C), gla_prev.astype(jnp.float32))
    mu = o.mean(-1, keepdims=True)
    var = jnp.square(o - mu).mean(-1, keepdims=True)
    o = ((o - mu) * lax.rsqrt(var + LN_EPS)).reshape(bsz, t_len, V_C)
    y_c = (o * g_gla * jax.nn.silu(r_c.astype(jnp.float32))).astype(x.dtype)
    gates = jax.nn.sigmoid(gate_logits.astype(jnp.float32)).astype(x.dtype).reshape(bsz, t_len, N_BRANCH, D_MODEL)
    merged = (gates[:, :, 0] * (y_a @ w_br_a) + gates[:, :, 1] * (y_b @ w_br_b)
              + gates[:, :, 2] * (y_c @ w_br_c))
    return merged @ w_out, k_a, v_a, conv_state, s_fin


def hier_moe(x, w_rg, b_rg, w_re, b_re, w_e_gate, w_e_up, w_e_down):
    shp = x.shape
    xf = x.reshape(-1, D_MODEL)
    n_tok = xf.shape[0]
    p_group = jax.nn.softmax((xf @ w_rg + b_rg).astype(jnp.float32), axis=-1)
    p_top, g_idx = lax.top_k(p_group, 1)
    e_logits = (xf @ w_re + b_re).astype(jnp.float32).reshape(n_tok, N_GROUPS, EXPERTS_PER_GROUP)
    e_logits = jnp.take_along_axis(e_logits, g_idx[:, :, None], axis=1)[:, 0]
    p_e, e_idx = lax.top_k(jax.nn.softmax(e_logits, axis=-1), TOP_E)
    w = p_top * p_e / p_e.sum(-1, keepdims=True)
    expert_id = g_idx * EXPERTS_PER_GROUP + e_idx
    combine = jnp.einsum('nk,nke->ne', w, jax.nn.one_hot(expert_id, N_EXPERTS, dtype=jnp.float32)).astype(x.dtype)
    h = jax.nn.silu(jnp.einsum('nd,edf->nef', xf, w_e_gate)) * jnp.einsum('nd,edf->nef', xf, w_e_up)
    y = jnp.einsum('nef,efd->nd', h * combine[:, :, None], w_e_down)
    return y.reshape(shp)


def trunk_layer(x, k_past, v_past, conv_prev, gla_prev, w_in, conv_w, w_fg2, b_fg, g_gla,
                w_br_a, w_br_b, w_br_c, w_out, ln1_g, ln1_b, ln2_g, ln2_b,
                w_rg, b_rg, w_re, b_re, w_e_gate, w_e_up, w_e_down):
    mix, k_new, v_new, conv_new, gla_new = token_mixers(
        x, k_past, v_past, conv_prev, gla_prev, w_in, conv_w, w_fg2, b_fg, g_gla,
        w_br_a, w_br_b, w_br_c, w_out)
    h = layer_norm(ALPHA * x + mix, ln1_g, ln1_b)
    y = layer_norm(ALPHA * h + hier_moe(h, w_rg, b_rg, w_re, b_re, w_e_gate, w_e_up, w_e_down), ln2_g, ln2_b)
    return y, k_new, v_new, conv_new, gla_new


def setup_inputs(seed: int = 0) -> dict:
    key = jax.random.key(seed)
    ks = jax.random.split(key, 27)
    f32 = jnp.float32
    n_pages = PAST_LEN // PAGE_SIZE
    used = DEC_BATCH * n_pages
    n_pool = used + used // 4 + 1

    def nrm(k, shape, scale):
        return jax.random.normal(k, shape, f32) * scale

    col_scale = jnp.concatenate([jnp.full((w,), s, f32) for w, s in zip(
        IN_SPLITS, (1.0, 1.0, BETA, 1.0, 1.0, BETA, 1.0, 1.0, BETA, 1.0, 1.0, 1.0))])
    page_table = jax.random.permutation(ks[6], n_pool)[:used].reshape(DEC_BATCH, n_pages).astype(jnp.int32)
    return {
        'x_prompt': nrm(ks[0], (BATCH, SEQ, D_MODEL), 1.0),
        'x_sample': nrm(ks[1], (DEC_BATCH, DEC_SEQ, D_MODEL), 1.0),
        'cache_k': nrm(ks[2], (DEPTH, n_pool, PAGE_SIZE, H_A, DH_A), 1.0),
        'cache_v': nrm(ks[3], (DEPTH, n_pool, PAGE_SIZE, H_A, DH_A), BETA),
        'state_conv': nrm(ks[4], (DEPTH, DEC_BATCH, CONV_W - 1, W_B), 0.5),
        'state_gla': nrm(ks[5], (DEPTH, DEC_BATCH, H_C, DK_C, DV_C), 1.0),
        'page_table': page_table,
        'w_in': nrm(ks[7], (DEPTH, D_MODEL, IN_TOTAL), D_MODEL ** -0.5) * col_scale,
        'conv_w': nrm(ks[8], (DEPTH, CONV_W, W_B), CONV_W ** -0.5),
        'w_fg2': nrm(ks[9], (DEPTH, FG_RANK, QK_C), FG_RANK ** -0.5),
        'b_fg': nrm(ks[10], (DEPTH, QK_C), 0.1),
        'g_gla': 1.0 + nrm(ks[11], (DEPTH, V_C), 0.02),
        'w_br_a': nrm(ks[12], (DEPTH, W_A, D_MODEL), BETA * W_A ** -0.5),
        'w_br_b': nrm(ks[13], (DEPTH, W_B, D_MODEL), BETA * W_B ** -0.5),
        'w_br_c': nrm(ks[14], (DEPTH, V_C, D_MODEL), BETA * V_C ** -0.5),
        'w_out': nrm(ks[15], (DEPTH, D_MODEL, D_MODEL), BETA * D_MODEL ** -0.5),
        'ln1_g': 1.0 + nrm(ks[16], (DEPTH, D_MODEL), 0.02),
        'ln1_b': nrm(ks[17], (DEPTH, D_MODEL), 0.02),
        'ln2_g': 1.0 + nrm(ks[18], (DEPTH, D_MODEL), 0.02),
        'ln2_b': nrm(ks[19], (DEPTH, D_MODEL), 0.02),
        'w_rg': nrm(ks[20], (DEPTH, D_MODEL, N_GROUPS), D_MODEL ** -0.5),
        'b_rg': nrm(ks[21], (DEPTH, N_GROUPS), 0.01),
        'w_re': nrm(ks[22], (DEPTH, D_MODEL, N_EXPERTS), D_MODEL ** -0.5),
        'b_re': nrm(ks[23], (DEPTH, N_EXPERTS), 0.01),
        'w_e_gate': nrm(ks[24], (DEPTH, N_EXPERTS, D_MODEL, D_EXPERT), BETA * D_MODEL ** -0.5),
        'w_e_up': nrm(ks[25], (DEPTH, N_EXPERTS, D_MODEL, D_EXPERT), BETA * D_MODEL ** -0.5),
        'w_e_down': nrm(ks[26], (DEPTH, N_EXPERTS, D_EXPERT, D_MODEL), BETA * D_EXPERT ** -0.5),
    }


def reference(x_prompt, x_sample, cache_k, cache_v, state_conv, state_gla, page_table,
              w_in, conv_w, w_fg2, b_fg, g_gla, w_br_a, w_br_b, w_br_c, w_out,
              ln1_g, ln1_b, ln2_g, ln2_b, w_rg, b_rg, w_re, b_re, w_e_gate, w_e_up, w_e_down):
    bsz = x_prompt.shape[0]
    dec_b = x_sample.shape[0]
    past = page_table.shape[1] * cache_k.shape[2]
    xp, xs = x_prompt, x_sample
    kp_l, vp_l, cp_l, sp_l = [], [], [], []
    ks_l, vs_l, cs_l, ss_l = [], [], [], []
    for l in range(DEPTH):
        params = (w_in[l], conv_w[l], w_fg2[l], b_fg[l], g_gla[l], w_br_a[l], w_br_b[l], w_br_c[l],
                  w_out[l], ln1_g[l], ln1_b[l], ln2_g[l], ln2_b[l], w_rg[l], b_rg[l], w_re[l], b_re[l],
                  w_e_gate[l], w_e_up[l], w_e_down[l])
        empty = jnp.zeros((bsz, 0, H_A, DH_A), xp.dtype)
        xp, kp, vp, cp, sp = trunk_layer(
            xp, empty, empty, jnp.zeros((bsz, CONV_W - 1, W_B), xp.dtype),
            jnp.zeros((bsz, H_C, DK_C, DV_C), jnp.float32), *params)
        k_past = cache_k[l, page_table].reshape(dec_b, past, H_A, DH_A)
        v_past = cache_v[l, page_table].reshape(dec_b, past, H_A, DH_A)
        xs, ksn, vsn, csn, ssn = trunk_layer(xs, k_past, v_past, state_conv[l], state_gla[l], *params)
        kp_l.append(kp); vp_l.append(vp); cp_l.append(cp); sp_l.append(sp)
        ks_l.append(ksn); vs_l.append(vsn); cs_l.append(csn); ss_l.append(ssn)
    return (xp, xs,
            jnp.stack(kp_l), jnp.stack(vp_l), jnp.stack(cp_l), jnp.stack(sp_l).astype(state_gla.dtype),
            jnp.stack(ks_l), jnp.stack(vs_l), jnp.stack(cs_l), jnp.stack(ss_l).astype(state_gla.dtype))
```

```python
import functools

import numpy as np
import jax
import jax.numpy as jnp
from jax import lax
from jax.experimental import pallas as pl
from jax.experimental.pallas import tpu as pltpu

D_MODEL = 1024
DEPTH = 4
H_A = 8
DH_A = 64
W_A = H_A * DH_A
MOBA_BLOCK = 256
MOBA_TOPK = 3
CONV_W = 3
W_B = 512
H_C = 4
DK_C = 64
DV_C = 128
QK_C = H_C * DK_C
V_C = H_C * DV_C
FG_RANK = 16
GATE_TEMP = 16.0
N_BRANCH = 3
N_GROUPS = 4
EXPERTS_PER_GROUP = 4
N_EXPERTS = N_GROUPS * EXPERTS_PER_GROUP
D_EXPERT = 512
ALPHA = (2 * DEPTH) ** 0.25
LN_EPS = 1e-5
NEG_INF = -1e30

LANES = 128
SUBLANES = 8
MIB = 1024 * 1024

GLA_CHUNK = 64
GLA_LEVELS = 6
GLA_TB = 512

_P_Q, _P_K, _P_V = 0, W_A, 2 * W_A
_P_B, _P_C, _P_H = 3 * W_A, 3 * W_A + W_B, 3 * W_A + 2 * W_B
_P_QC = 3 * W_A + 3 * W_B
_P_KC = _P_QC + QK_C
_P_VC = _P_KC + QK_C
_P_RC = _P_VC + V_C
_P_FL = _P_RC + V_C
_P_TOT = _P_FL + LANES

bf16 = jnp.bfloat16
f32 = jnp.float32


def _cparams(sem, vmem_mib):
    return pltpu.CompilerParams(dimension_semantics=sem, vmem_limit_bytes=vmem_mib * MIB)


def _dot(a, b):
    return jnp.dot(a, b, preferred_element_type=f32)


def _dot_nt(a, b):
    return lax.dot_general(a, b, (((1,), (1,)), ((), ())), preferred_element_type=f32)


def _layer_norm_rows(x, g, b):
    mu = jnp.mean(x, axis=-1, keepdims=True)
    xc = x - mu
    var = jnp.mean(xc * xc, axis=-1, keepdims=True)
    return xc * lax.rsqrt(var + LN_EPS) * g + b


def _silu(x):
    return x * jax.nn.sigmoid(x)


def _proj_kernel(seq_tiles, *refs):
    if seq_tiles:
        (x_ref, w_ref, wfg_ref, bfg_ref, cw_ref,
         q_ref, k_ref, v_ref, yb_ref, u_ref, qk_ref, vc_ref, rc_ref, la_ref, carry_ref) = refs
    else:
        (x_ref, w_ref, wfg_ref, bfg_ref, cw_ref, p0_ref, p1_ref,
         q_ref, k_ref, v_ref, yb_ref, u_ref, qk_ref, vc_ref, rc_ref, la_ref) = refs
    x = x_ref[...]
    tm = x.shape[0]

    def cols(a, n):
        return _dot(x, w_ref[:, a:a + n])

    q_ref[...] = (cols(_P_Q, W_A) * (DH_A ** -0.5)).astype(q_ref.dtype)
    k_ref[...] = cols(_P_K, W_A)
    v_ref[...] = cols(_P_V, W_A)

    u = cols(_P_C, W_B) * cols(_P_H, W_B)
    cw = cw_ref[...]
    if seq_tiles:
        i = pl.program_id(0)

        @pl.when(i % seq_tiles == 0)
        def _():
            carry_ref[...] = jnp.zeros_like(carry_ref)

        c6 = carry_ref[SUBLANES - 2:SUBLANES - 1, :]
        c7 = carry_ref[SUBLANES - 1:SUBLANES, :]
        row = lax.broadcasted_iota(jnp.int32, u.shape, 0)
        u1 = jnp.where(row >= 1, pltpu.roll(u, 1, 0), c7)
        u2 = jnp.where(row >= 2, pltpu.roll(u, 2, 0), jnp.where(row == 1, c7, c6))
        tail = u[tm - SUBLANES:, :]
        carry_ref[...] = tail
        u_ref[...] = tail
    else:
        u2 = p0_ref[...]
        u1 = p1_ref[...]
        u_ref[...] = u
    conv = cw[0:1, :] * u2 + cw[1:2, :] * u1 + cw[2:3, :] * u
    yb_ref[...] = (cols(_P_B, W_B) * conv).astype(yb_ref.dtype)

    qk_ref[:, 0:QK_C] = cols(_P_QC, QK_C) * (DK_C ** -0.5)
    qk_ref[:, QK_C:2 * QK_C] = cols(_P_KC, QK_C)
    vc_ref[...] = cols(_P_VC, V_C)
    rc_ref[...] = cols(_P_RC, V_C)
    fl = cols(_P_FL, LANES)
    f = _dot(fl.astype(bf16), wfg_ref[...]) + bfg_ref[...]
    la_ref[...] = (jnp.minimum(f, 0.0) - jnp.log1p(jnp.exp(-jnp.abs(f)))) * (1.0 / GATE_TEMP)


def _proj(x_bf, w_proj, wfg, bfg, conv_w, tm, seq_tiles, prev=None):
    m = x_bf.shape[0]
    n_tiles = m // tm
    row = lambda n: pl.BlockSpec((tm, n), lambda i: (i, 0))
    full = lambda a: pl.BlockSpec(a.shape, lambda i: (0,) * a.ndim)
    in_specs = [row(D_MODEL), full(w_proj), full(wfg), full(bfg), full(conv_w)]
    args = [x_bf, w_proj, wfg, bfg, conv_w]
    scratch = []
    if seq_tiles:
        u_shape, u_spec = (n_tiles * SUBLANES, W_B), pl.BlockSpec((SUBLANES, W_B), lambda i: (i, 0))
        scratch = [pltpu.VMEM((SUBLANES, W_B), f32)]
    else:
        u_shape, u_spec = (m, W_B), row(W_B)
        in_specs += [row(W_B), row(W_B)]
        args += [prev[0], prev[1]]
    out_shape = [
        jax.ShapeDtypeStruct((m, W_A), bf16),
        jax.ShapeDtypeStruct((m, W_A), f32),
        jax.ShapeDtypeStruct((m, W_A), f32),
        jax.ShapeDtypeStruct((m, W_B), bf16),
        jax.ShapeDtypeStruct(u_shape, f32),
        jax.ShapeDtypeStruct((m, 2 * QK_C), f32),
        jax.ShapeDtypeStruct((m, V_C), f32),
        jax.ShapeDtypeStruct((m, V_C), f32),
        jax.ShapeDtypeStruct((m, QK_C), f32),
    ]
    out_specs = [row(W_A), row(W_A), row(W_A), row(W_B), u_spec, row(2 * QK_C), row(V_C), row(V_C), row(QK_C)]
    return pl.pallas_call(
        functools.partial(_proj_kernel, seq_tiles),
        grid=(n_tiles,),
        in_specs=in_specs,
        out_specs=out_specs,
        out_shape=out_shape,
        scratch_shapes=scratch,
        compiler_params=_cparams(("arbitrary",), 56),
        name="proj",
    )(*args)


def _attn_prompt_kernel(q_ref, k_ref, v_ref, o_ref, kmean_ref, sel_ref, m_ref, l_ref, acc_ref):
    qi = pl.program_id(2)
    nb = kmean_ref.shape[0]
    bl = MOBA_BLOCK

    @pl.when(qi == 0)
    def _():
        for j in range(nb):
            kmean_ref[j:j + 1, :] = jnp.mean(k_ref[j * bl:(j + 1) * bl, :], axis=0, keepdims=True)

    q = q_ref[...]
    lane = lax.broadcasted_iota(jnp.int32, q.shape, 1)
    head_lo = lane < DH_A
    qh = (jnp.where(head_lo, q, jnp.zeros_like(q)), jnp.where(head_lo, jnp.zeros_like(q), q))
    km = kmean_ref[...].astype(bf16)
    blk = lax.broadcasted_iota(jnp.int32, (bl, nb), 1)
    n_valid = jnp.minimum(qi, MOBA_TOPK)

    row = lax.broadcasted_iota(jnp.int32, (bl, bl), 0)
    col = lax.broadcasted_iota(jnp.int32, (bl, bl), 1)
    own0 = pl.multiple_of(qi * bl, bl)
    k_own = k_ref[pl.ds(own0, bl), :].astype(bf16)
    v_own = v_ref[pl.ds(own0, bl), :].astype(bf16)
    for hh in range(2):
        s = jnp.where(blk < qi, _dot_nt(qh[hh], km), NEG_INF)
        chosen = jnp.zeros(s.shape, f32)
        for r in range(MOBA_TOPK):
            mx = jnp.max(s, axis=1, keepdims=True)
            first = jnp.min(jnp.where(s == mx, blk, nb), axis=1, keepdims=True)
            pick = blk == first
            chosen = jnp.where(pick, jnp.where(r < n_valid, 1.0, 0.0), chosen)
            s = jnp.where(pick, -jnp.inf, s)
        sel_ref[hh] = chosen
        so = jnp.where(row >= col, _dot_nt(qh[hh], k_own), NEG_INF)
        mo = jnp.max(so, axis=1, keepdims=True)
        p = jnp.exp(so - mo)
        m_ref[hh] = mo
        l_ref[hh] = jnp.sum(p, axis=1, keepdims=True)
        acc_ref[hh] = _dot(p.astype(bf16), v_own)

    def past_block(j, carry):
        j0 = pl.multiple_of(j * bl, bl)
        for hh in range(2):
            colsel = jnp.sum(jnp.where(blk == j, sel_ref[hh], 0.0), axis=1, keepdims=True)

            @pl.when(jnp.max(colsel) > 0.0)
            def _():
                kj = k_ref[pl.ds(j0, bl), :].astype(bf16)
                vj = v_ref[pl.ds(j0, bl), :].astype(bf16)
                s = jnp.where(colsel > 0.0, _dot_nt(qh[hh], kj), NEG_INF)
                m_old = m_ref[hh]
                m_new = jnp.maximum(m_old, jnp.max(s, axis=1, keepdims=True))
                p = jnp.exp(s - m_new)
                a = jnp.exp(m_old - m_new)
                l_ref[hh] = a * l_ref[hh] + jnp.sum(p, axis=1, keepdims=True)
                acc_ref[hh] = a * acc_ref[hh] + _dot(p.astype(bf16), vj)
                m_ref[hh] = m_new
        return carry

    lax.fori_loop(0, qi, past_block, 0)
    o_ref[...] = jnp.where(head_lo, acc_ref[0] / l_ref[0], acc_ref[1] / l_ref[1]).astype(o_ref.dtype)


def _attn_prompt(q, k, v, bsz, t_len):
    nb = t_len // MOBA_BLOCK
    bl = MOBA_BLOCK
    pair = 2 * DH_A
    kv_spec = pl.BlockSpec((t_len, pair), lambda b, hp, qi: (b, hp))
    q_spec = pl.BlockSpec((bl, pair), lambda b, hp, qi: (b * nb + qi, hp))
    return pl.pallas_call(
        _attn_prompt_kernel,
        grid=(bsz, W_A // pair, nb),
        in_specs=[q_spec, kv_spec, kv_spec],
        out_specs=q_spec,
        out_shape=jax.ShapeDtypeStruct(q.shape, bf16),
        scratch_shapes=[
            pltpu.VMEM((nb, pair), f32),
            pltpu.VMEM((2, bl, nb), f32),
            pltpu.VMEM((2, bl, 1), f32),
            pltpu.VMEM((2, bl, 1), f32),
            pltpu.VMEM((2, bl, pair), f32),
        ],
        compiler_params=_cparams(("arbitrary", "arbitrary", "arbitrary"), 40),
        name="attn_prompt",
    )(q, k, v)


def _attn_select_kernel(layer, pt_ref, q_ref, ind_ref, cache_ref, idx_ref, kbuf_ref, sem_ref):
    b = pl.program_id(0)
    n_pages, page = kbuf_ref.shape[0], kbuf_ref.shape[1]
    per_block = MOBA_BLOCK // page
    nbk = n_pages // per_block

    def page_copy(p):
        return pltpu.make_async_copy(cache_ref.at[layer, pt_ref[b, p]], kbuf_ref.at[p], sem_ref.at[0])

    for p in range(n_pages):
        page_copy(p).start()
    for p in range(n_pages):
        page_copy(p).wait()

    rows = []
    for j in range(nbk):
        acc = jnp.sum(kbuf_ref[j * per_block], axis=0, keepdims=True)
        for t in range(1, per_block):
            acc = acc + jnp.sum(kbuf_ref[j * per_block + t], axis=0, keepdims=True)
        rows.append(acc)
    kmean = jnp.concatenate(rows, axis=0) * (1.0 / MOBA_BLOCK)
    prod = kmean * q_ref[0]
    hi = prod.astype(bf16)
    lo = (prod - hi.astype(f32)).astype(bf16)
    score = _dot(hi, ind_ref[...]) + _dot(lo, ind_ref[...])
    blk = lax.broadcasted_iota(jnp.int32, score.shape, 0)
    out_row = lax.broadcasted_iota(jnp.int32, (SUBLANES, LANES), 0)
    idx = jnp.zeros((SUBLANES, LANES), jnp.int32)
    for r in range(MOBA_TOPK):
        mx = jnp.max(score, axis=0, keepdims=True)
        first = jnp.min(jnp.where(score == mx, blk, nbk), axis=0, keepdims=True)
        idx = jnp.where(out_row == r, first, idx)
        score = jnp.where(blk == first, -jnp.inf, score)
    idx_ref[0] = idx


def _attn_select(page_table, q3, head_ind, cache_k, layer):
    bsz, n_pages = page_table.shape
    page = cache_k.shape[2]
    grid_spec = pltpu.PrefetchScalarGridSpec(
        num_scalar_prefetch=1,
        grid=(bsz,),
        in_specs=[
            pl.BlockSpec((1, 1, W_A), lambda b, pt: (b, 0, 0)),
            pl.BlockSpec(head_ind.shape, lambda b, pt: (0, 0)),
            pl.BlockSpec(memory_space=pl.ANY),
        ],
        out_specs=pl.BlockSpec((1, SUBLANES, LANES), lambda b, pt: (b, 0, 0)),
        scratch_shapes=[pltpu.VMEM((n_pages, page, W_A), f32), pltpu.SemaphoreType.DMA((1,))],
    )
    return pl.pallas_call(
        functools.partial(_attn_select_kernel, layer),
        grid_spec=grid_spec,
        out_shape=jax.ShapeDtypeStruct((bsz, SUBLANES, LANES), jnp.int32),
        compiler_params=_cparams(("arbitrary",), 40),
        name="attn_select",
    )(page_table, q3, head_ind, cache_k)


def _attn_sample_kernel(layer, pt_ref, sel_ref, q_ref, kn_ref, vn_ref, ck_ref, cv_ref, o_ref,
                        kbuf_ref, vbuf_ref, sem_ref):
    b = pl.program_id(0)
    page = ck_ref.shape[2]
    per_block = MOBA_BLOCK // page

    def copies(h, r, t):
        pg = pt_ref[b, sel_ref[b, r * H_A + h] * per_block + t]
        dst = pl.ds((r * per_block + t) * page, page)
        return (pltpu.make_async_copy(ck_ref.at[layer, pg, :, h, :], kbuf_ref.at[h, dst, :], sem_ref.at[0]),
                pltpu.make_async_copy(cv_ref.at[layer, pg, :, h, :], vbuf_ref.at[h, dst, :], sem_ref.at[1]))

    slots = [(h, r, t) for h in range(H_A) for r in range(MOBA_TOPK) for t in range(per_block)]
    for s in slots:
        ck, cv = copies(*s)
        ck.start()
        cv.start()
    for s in slots:
        ck, cv = copies(*s)
        ck.wait()
        cv.wait()

    for h in range(H_A):
        qh = q_ref[0, h:h + 1, :]
        kh = kbuf_ref[h].astype(bf16)
        vh = vbuf_ref[h].astype(bf16)
        q8 = jnp.broadcast_to(qh, (SUBLANES, DH_A)).astype(bf16)
        s = _dot_nt(q8, kh)
        s_own = jnp.sum(qh * kn_ref[0, h:h + 1, :], axis=1, keepdims=True)
        m = jnp.maximum(jnp.max(s, axis=1, keepdims=True), s_own)
        p = jnp.exp(s - m)
        p_own = jnp.exp(s_own - m)
        den = jnp.sum(p, axis=1, keepdims=True) + p_own
        out = (_dot(p.astype(bf16), vh) + p_own * vn_ref[0, h:h + 1, :]) / den
        o_ref[0, h:h + 1, :] = out[0:1, :].astype(o_ref.dtype)


def _attn_sample(page_table, sel, q, k_new, v_new, cache_k, cache_v, layer):
    bsz = page_table.shape[0]
    n_sel = MOBA_TOPK * MOBA_BLOCK
    tok = pl.BlockSpec((1, H_A, DH_A), lambda b, pt, s: (b, 0, 0))
    grid_spec = pltpu.PrefetchScalarGridSpec(
        num_scalar_prefetch=2,
        grid=(bsz,),
        in_specs=[tok, tok, tok, pl.BlockSpec(memory_space=pl.ANY), pl.BlockSpec(memory_space=pl.ANY)],
        out_specs=tok,
        scratch_shapes=[pltpu.VMEM((H_A, n_sel, DH_A), f32), pltpu.VMEM((H_A, n_sel, DH_A), f32),
                        pltpu.SemaphoreType.DMA((2,))],
    )
    return pl.pallas_call(
        functools.partial(_attn_sample_kernel, layer),
        grid_spec=grid_spec,
        out_shape=jax.ShapeDtypeStruct((bsz, H_A, DH_A), bf16),
        compiler_params=_cparams(("arbitrary",), 32),
        name="attn_sample",
    )(page_table, sel, q, k_new, v_new, cache_k, cache_v)


def _gla_tables():
    c = GLA_CHUNK
    tri = np.tril(np.ones((c, c), np.float32))
    mats = [tri]
    masks = []
    t = np.arange(c)
    for lev in range(GLA_LEVELS):
        h = 1 << lev
        mid = (t // (2 * h)) * (2 * h) + h
        mats.append(tri[mid - 1])
        same = (t[:, None] // (2 * h)) == (t[None, :] // (2 * h))
        masks.append(same & ((t[:, None] & h) != 0) & ((t[None, :] & h) == 0))
    masks.append(np.eye(c, dtype=bool))
    sel = np.concatenate(mats, axis=0)
    lmask = np.stack([np.tile(mk, (H_C, 1)) for mk in masks]).astype(np.float32)
    return sel, lmask


def _gla_prompt_kernel(qk_ref, la_ref, v_ref, r_ref, g_ref, s0_ref, sel_ref, lmask_ref,
                       y_ref, sout_ref, s_ref):
    c = GLA_CHUNK

    @pl.when(pl.program_id(1) == 0)
    def _():
        s_ref[...] = s0_ref[0]

    sel = sel_ref[...]
    g = g_ref[...]
    lane = lax.broadcasted_iota(jnp.int32, (c, QK_C), 1)
    rowc = lax.broadcasted_iota(jnp.int32, (c, QK_C), 0)
    head_of_lane = lane // DK_C
    head_of_row = lax.broadcasted_iota(jnp.int32, (QK_C, DV_C), 0) // DK_C

    def stack_heads(a):
        return jnp.concatenate([jnp.where(head_of_lane == h, a, 0.0) for h in range(H_C)], axis=0)

    def chunk(ci, carry):
        r0 = pl.multiple_of(ci * c, c)
        la = la_ref[pl.ds(r0, c), :]
        la1 = la.astype(bf16)
        rem = la - la1.astype(f32)
        la2 = rem.astype(bf16)
        la3 = (rem - la2.astype(f32)).astype(bf16)
        ball = _dot(sel, la1) + _dot(sel, la2) + _dot(sel, la3)
        b = ball[0:c]
        q = qk_ref[pl.ds(r0, c), 0:QK_C]
        k = qk_ref[pl.ds(r0, c), QK_C:2 * QK_C]
        v = v_ref[pl.ds(r0, c), :].astype(bf16)

        att = lmask_ref[GLA_LEVELS] * _dot_nt(stack_heads(q).astype(bf16), k.astype(bf16))
        for lev in range(GLA_LEVELS):
            ref_b = ball[(lev + 1) * c:(lev + 2) * c]
            right = (rowc & (1 << lev)) != 0
            qt = q * jnp.exp(jnp.where(right, b - ref_b, NEG_INF))
            kt = k * jnp.exp(jnp.where(right, NEG_INF, ref_b - b))
            att = att + lmask_ref[lev] * _dot_nt(stack_heads(qt).astype(bf16), kt.astype(bf16))

        s_old = s_ref[...]
        inter = _dot(stack_heads(q * jnp.exp(b)).astype(bf16), s_old.astype(bf16))
        b_last = b[c - 1:c, :]
        kt_t = jnp.transpose(k * jnp.exp(b_last - b))
        b_t = jnp.transpose(b)
        upd = _dot(kt_t.astype(bf16), v)
        s_new = jnp.exp(b_t[:, c - 1:c]) * s_old
        for h in range(H_C):
            s_new = s_new + jnp.where(head_of_row == h, upd[:, h * DV_C:(h + 1) * DV_C], 0.0)
        s_ref[...] = s_new

        att_bf = att.astype(bf16)
        for h in range(H_C):
            o = (_dot(att_bf[h * c:(h + 1) * c], v[:, h * DV_C:(h + 1) * DV_C])
                 + inter[h * c:(h + 1) * c])
            mu = jnp.mean(o, axis=-1, keepdims=True)
            oc = o - mu
            var = jnp.mean(oc * oc, axis=-1, keepdims=True)
            hs = slice(h * DV_C, (h + 1) * DV_C)
            gate = g[:, hs] * _silu(r_ref[pl.ds(r0, c), hs])
            y_ref[pl.ds(r0, c), hs] = (oc * lax.rsqrt(var + LN_EPS) * gate).astype(y_ref.dtype)
        return carry

    lax.fori_loop(0, qk_ref.shape[0] // c, chunk, 0)
    sout_ref[0] = s_ref[...]


def _gla_prompt(qk, la, v, r, g_gla, s0, bsz, t_len):
    tb = min(GLA_TB, t_len)
    nt = t_len // tb
    sel_np, lmask_np = _gla_tables()
    sel = jnp.asarray(sel_np, bf16)
    lmask = jnp.asarray(lmask_np, f32)
    row = lambda n: pl.BlockSpec((tb, n), lambda b, t: (b * nt + t, 0))
    st = pl.BlockSpec((1, QK_C, DV_C), lambda b, t: (b, 0, 0))
    return pl.pallas_call(
        _gla_prompt_kernel,
        grid=(bsz, nt),
        in_specs=[row(2 * QK_C), row(QK_C), row(V_C), row(V_C),
                  pl.BlockSpec((1, V_C), lambda b, t: (0, 0)), st,
                  pl.BlockSpec(sel.shape, lambda b, t: (0, 0)),
                  pl.BlockSpec(lmask.shape, lambda b, t: (0, 0, 0))],
        out_specs=[row(V_C), st],
        out_shape=[jax.ShapeDtypeStruct((bsz * t_len, V_C), bf16),
                   jax.ShapeDtypeStruct((bsz, QK_C, DV_C), f32)],
        scratch_shapes=[pltpu.VMEM((QK_C, DV_C), f32)],
        compiler_params=_cparams(("arbitrary", "arbitrary"), 40),
        name="gla_prompt",
    )(qk, la, v, r, g_gla, s0, sel, lmask)


def _gla_step_kernel(q_ref, k_ref, la_ref, v_ref, r_ref, g_ref, s_ref, y_ref, sout_ref):
    head_of_row = lax.broadcasted_iota(jnp.int32, (QK_C, DV_C), 0) // DK_C
    v = v_ref[0]
    v_rows = jnp.zeros((QK_C, DV_C), f32)
    for h in range(H_C):
        v_rows = jnp.where(head_of_row == h, v[:, h * DV_C:(h + 1) * DV_C], v_rows)
    s_new = jnp.exp(la_ref[0]) * s_ref[0] + k_ref[0] * v_rows
    sout_ref[0] = s_new
    qs = q_ref[0] * s_new
    g = g_ref[...]
    for h in range(H_C):
        o = jnp.sum(qs[h * DK_C:(h + 1) * DK_C], axis=0, keepdims=True)
        mu = jnp.mean(o, axis=-1, keepdims=True)
        oc = o - mu
        var = jnp.mean(oc * oc, axis=-1, keepdims=True)
        hs = slice(h * DV_C, (h + 1) * DV_C)
        y_ref[0, :, hs] = (oc * lax.rsqrt(var + LN_EPS) * g[:, hs] * _silu(r_ref[0, :, hs])).astype(y_ref.dtype)


def _gla_step(q_col, k_col, la_col, v, r, g_gla, s):
    bsz = s.shape[0]
    col = pl.BlockSpec((1, QK_C, 1), lambda b: (b, 0, 0))
    tok = pl.BlockSpec((1, 1, V_C), lambda b: (b, 0, 0))
    st = pl.BlockSpec((1, QK_C, DV_C), lambda b: (b, 0, 0))
    return pl.pallas_call(
        _gla_step_kernel,
        grid=(bsz,),
        in_specs=[col, col, col, tok, tok, pl.BlockSpec((1, V_C), lambda b: (0, 0)), st],
        out_specs=[tok, st],
        out_shape=[jax.ShapeDtypeStruct((bsz, 1, V_C), bf16), jax.ShapeDtypeStruct(s.shape, f32)],
        compiler_params=_cparams(("arbitrary",), 16),
        name="gla_step",
    )(q_col, k_col, la_col, v, r, g_gla, s)


def _merge_kernel(xb_ref, xf_ref, ya_ref, yb_ref, yc_ref, wg_ref, wa_ref, wb_ref, wc_ref, wo_ref,
                  g_ref, b_ref, hf_ref, hb_ref):
    xb = xb_ref[...]
    merged = None
    for i, (y_ref, w_ref) in enumerate(((ya_ref, wa_ref), (yb_ref, wb_ref), (yc_ref, wc_ref))):
        gate = jax.nn.sigmoid(_dot(xb, wg_ref[:, i * D_MODEL:(i + 1) * D_MODEL]))
        term = gate * _dot(y_ref[...], w_ref[...])
        merged = term if merged is None else merged + term
    mix = _dot(merged.astype(bf16), wo_ref[...])
    h = _layer_norm_rows(ALPHA * xf_ref[...] + mix, g_ref[...], b_ref[...])
    hf_ref[...] = h
    hb_ref[...] = h.astype(bf16)


def _merge(x_bf, x_f, y_a, y_b, y_c, w_gate, w_a, w_b, w_c, w_o, ln_g, ln_b, tm):
    m = x_bf.shape[0]
    row = lambda n: pl.BlockSpec((tm, n), lambda i: (i, 0))
    full = lambda a: pl.BlockSpec(a.shape, lambda i: (0,) * a.ndim)
    return pl.pallas_call(
        _merge_kernel,
        grid=(m // tm,),
        in_specs=[row(D_MODEL), row(D_MODEL), row(W_A), row(W_B), row(V_C),
                  full(w_gate), full(w_a), full(w_b), full(w_c), full(w_o), full(ln_g), full(ln_b)],
        out_specs=[row(D_MODEL), row(D_MODEL)],
        out_shape=[jax.ShapeDtypeStruct((m, D_MODEL), f32), jax.ShapeDtypeStruct((m, D_MODEL), bf16)],
        compiler_params=_cparams(("arbitrary",), 56),
        name="merge",
    )(x_bf, x_f, y_a, y_b, y_c, w_gate, w_a, w_b, w_c, w_o, ln_g, ln_b)


def _moe_kernel(hb_ref, hf_ref, wr_ref, br_ref, wg_ref, wu_ref, wd_ref, g_ref, b_ref,
                yf_ref, yb_ref, comb_ref, acc_ref):
    e = pl.program_id(1)
    hb = hb_ref[...]

    @pl.when(e == 0)
    def _():
        logits = _dot(hb, wr_ref[...]) + br_ref[...]
        lane = lax.broadcasted_iota(jnp.int32, logits.shape, 1)
        gl = jnp.where(lane < N_GROUPS, logits, -jnp.inf)
        gmax = jnp.max(gl, axis=1, keepdims=True)
        pg = jnp.exp(gl - gmax)
        pg = pg / jnp.sum(pg, axis=1, keepdims=True)
        p_top = jnp.max(pg, axis=1, keepdims=True)
        g_idx = jnp.min(jnp.where(pg == p_top, lane, LANES), axis=1, keepdims=True)
        e0 = N_GROUPS + g_idx * EXPERTS_PER_GROUP
        el = jnp.where((lane >= e0) & (lane < e0 + EXPERTS_PER_GROUP), logits, -jnp.inf)
        pe = jnp.exp(el - jnp.max(el, axis=1, keepdims=True))
        pe = pe / jnp.sum(pe, axis=1, keepdims=True)
        p1 = jnp.max(pe, axis=1, keepdims=True)
        i1 = jnp.min(jnp.where(pe == p1, lane, LANES), axis=1, keepdims=True)
        rest = jnp.where(lane == i1, -1.0, pe)
        p2 = jnp.max(rest, axis=1, keepdims=True)
        i2 = jnp.min(jnp.where(rest == p2, lane, LANES), axis=1, keepdims=True)
        den = p1 + p2
        comb_ref[...] = jnp.where(lane == i1, p_top * p1 / den, jnp.where(lane == i2, p_top * p2 / den, 0.0))
        acc_ref[...] = jnp.zeros_like(acc_ref)

    comb = comb_ref[...]
    lane = lax.broadcasted_iota(jnp.int32, comb.shape, 1)
    cw = jnp.sum(jnp.where(lane == N_GROUPS + e, comb, 0.0), axis=1, keepdims=True)
    act = _silu(_dot(hb, wg_ref[0])) * _dot(hb, wu_ref[0]) * cw
    acc_ref[...] += _dot(act.astype(bf16), wd_ref[0])

    @pl.when(e == N_EXPERTS - 1)
    def _():
        y = _layer_norm_rows(ALPHA * hf_ref[...] + acc_ref[...], g_ref[...], b_ref[...])
        yf_ref[...] = y
        yb_ref[...] = y.astype(bf16)


def _moe(h_bf, h_f, w_router, b_router, w_gate, w_up, w_down, ln_g, ln_b, tm):
    m = h_bf.shape[0]
    row = lambda n: pl.BlockSpec((tm, n), lambda i, e: (i, 0))
    full = lambda a: pl.BlockSpec(a.shape, lambda i, e: (0,) * a.ndim)
    return pl.pallas_call(
        _moe_kernel,
        grid=(m // tm, N_EXPERTS),
        in_specs=[row(D_MODEL), row(D_MODEL), full(w_router), full(b_router),
                  pl.BlockSpec((1, D_MODEL, D_EXPERT), lambda i, e: (e, 0, 0)),
                  pl.BlockSpec((1, D_MODEL, D_EXPERT), lambda i, e: (e, 0, 0)),
                  pl.BlockSpec((1, D_EXPERT, D_MODEL), lambda i, e: (e, 0, 0)),
                  full(ln_g), full(ln_b)],
        out_specs=[row(D_MODEL), row(D_MODEL)],
        out_shape=[jax.ShapeDtypeStruct((m, D_MODEL), f32), jax.ShapeDtypeStruct((m, D_MODEL), bf16)],
        scratch_shapes=[pltpu.VMEM((tm, LANES), f32), pltpu.VMEM((tm, D_MODEL), f32)],
        compiler_params=_cparams(("arbitrary", "arbitrary"), 48),
        name="moe",
    )(h_bf, h_f, w_router, b_router, w_gate, w_up, w_down, ln_g, ln_b)


def _row_tile(m):
    for tm in (512, 256, 128, 64, 32, 16, 8):
        if m % tm == 0:
            return tm
    raise ValueError(f"row count {m} must be a multiple of {SUBLANES}")


def _layer_weights(l, w_in, conv_w, w_fg2, b_fg, g_gla, w_br_a, w_br_b, w_br_c, w_out,
                   ln1_g, ln1_b, ln2_g, ln2_b, w_rg, b_rg, w_re, b_re, w_e_gate, w_e_up, w_e_down):
    wi = w_in[l]
    n_front = _P_FL
    gate0 = n_front + FG_RANK
    w_proj = jnp.concatenate(
        [wi[:, :n_front], jnp.pad(wi[:, n_front:gate0], ((0, 0), (0, LANES - FG_RANK)))], axis=1).astype(bf16)
    pad_r = LANES - N_GROUPS - N_EXPERTS
    return dict(
        w_proj=w_proj,
        w_gate=wi[:, gate0:].astype(bf16),
        wfg=jnp.pad(w_fg2[l], ((0, LANES - FG_RANK), (0, 0))).astype(bf16),
        bfg=b_fg[l][None, :],
        conv_w=conv_w[l],
        g_gla=g_gla[l][None, :],
        w_a=w_br_a[l].astype(bf16), w_b=w_br_b[l].astype(bf16), w_c=w_br_c[l].astype(bf16),
        w_o=w_out[l].astype(bf16),
        ln1_g=ln1_g[l][None, :], ln1_b=ln1_b[l][None, :], ln2_g=ln2_g[l][None, :], ln2_b=ln2_b[l][None, :],
        w_router=jnp.pad(jnp.concatenate([w_rg[l], w_re[l]], axis=1), ((0, 0), (0, pad_r))).astype(bf16),
        b_router=jnp.pad(jnp.concatenate([b_rg[l], b_re[l]]), (0, pad_r))[None, :],
        w_e_gate=w_e_gate[l].astype(bf16), w_e_up=w_e_up[l].astype(bf16), w_e_down=w_e_down[l].astype(bf16),
    )


def _finish_layer(p, x_bf, x_f, y_a, y_b, y_c, tm):
    h_f, h_bf = _merge(x_bf, x_f, y_a, y_b, y_c, p["w_gate"], p["w_a"], p["w_b"], p["w_c"], p["w_o"],
                       p["ln1_g"], p["ln1_b"], tm)
    return _moe(h_bf, h_f, p["w_router"], p["b_router"], p["w_e_gate"], p["w_e_up"], p["w_e_down"],
                p["ln2_g"], p["ln2_b"], tm)


def _prompt_layer(p, x_f, x_bf, bsz, t_len):
    tm = _row_tile(t_len)
    q, k, v, y_b, u_tail, qk, vc, rc, la = _proj(
        x_bf, p["w_proj"], p["wfg"], p["bfg"], p["conv_w"], tm, t_len // tm)
    y_a = _attn_prompt(q, k, v, bsz, t_len)
    y_c, s_fin = _gla_prompt(qk, la, vc, rc, p["g_gla"], jnp.zeros((bsz, QK_C, DV_C), f32), bsz, t_len)
    x_f, x_bf = _finish_layer(p, x_bf, x_f, y_a, y_b, y_c, tm)
    conv_state = u_tail.reshape(bsz, t_len // tm, SUBLANES, W_B)[:, -1, SUBLANES - (CONV_W - 1):, :]
    return (x_f, x_bf, k.reshape(bsz, t_len, H_A, DH_A), v.reshape(bsz, t_len, H_A, DH_A), conv_state,
            s_fin.reshape(bsz, H_C, DK_C, DV_C))


def _sample_layer(p, layer, x_f, x_bf, cache_k, cache_v, conv_prev, gla_prev, page_table, head_ind):
    bsz = x_f.shape[0]
    depth, n_pool, page = cache_k.shape[:3]
    q, k, v, y_b, u, qk, vc, rc, la = _proj(
        x_bf, p["w_proj"], p["wfg"], p["bfg"], p["conv_w"], bsz, 0, prev=(conv_prev[:, 0], conv_prev[:, 1]))
    idx = _attn_select(page_table, q.astype(f32)[:, None, :], head_ind,
                       cache_k.reshape(depth, n_pool, page, W_A), layer)
    sel = idx[:, :MOBA_TOPK, :H_A].reshape(bsz, MOBA_TOPK * H_A)
    heads = lambda a: a.reshape(bsz, H_A, DH_A)
    y_a = _attn_sample(page_table, sel, heads(q.astype(f32)), heads(k), heads(v), cache_k, cache_v, layer)
    y_c, s_new = _gla_step(qk[:, :QK_C, None], qk[:, QK_C:, None], la[:, :, None], vc[:, None, :],
                           rc[:, None, :], p["g_gla"], gla_prev.reshape(bsz, QK_C, DV_C))
    x_f, x_bf = _finish_layer(p, x_bf, x_f, y_a.reshape(bsz, W_A), y_b, y_c.reshape(bsz, V_C), bsz)
    conv_state = jnp.stack([conv_prev[:, 1], u], axis=1)
    return (x_f, x_bf, k.reshape(bsz, 1, H_A, DH_A), v.reshape(bsz, 1, H_A, DH_A), conv_state,
            s_new.reshape(bsz, H_C, DK_C, DV_C))


def kernel(x_prompt, x_sample, cache_k, cache_v, state_conv, state_gla, page_table, w_in, conv_w, w_fg2, b_fg, g_gla, w_br_a, w_br_b, w_br_c, w_out, ln1_g, ln1_b, ln2_g, ln2_b, w_rg, b_rg, w_re, b_re, w_e_gate, w_e_up, w_e_down):
    bsz, t_len, _ = x_prompt.shape
    dec_b = x_sample.shape[0]
    assert x_sample.shape[1] == 1 and t_len % MOBA_BLOCK == 0 and dec_b % SUBLANES == 0
    assert (page_table.shape[1] * cache_k.shape[2]) % MOBA_BLOCK == 0 and MOBA_BLOCK % cache_k.shape[2] == 0
    head_ind = jnp.asarray(
        (np.arange(W_A)[:, None] // DH_A) == np.arange(LANES)[None, :], bf16)
    xp_f = x_prompt.reshape(bsz * t_len, D_MODEL)
    xs_f = x_sample.reshape(dec_b, D_MODEL)
    xp_bf, xs_bf = xp_f.astype(bf16), xs_f.astype(bf16)
    outs_p, outs_s = [], []
    for l in range(DEPTH):
        p = _layer_weights(l, w_in, conv_w, w_fg2, b_fg, g_gla, w_br_a, w_br_b, w_br_c, w_out,
                           ln1_g, ln1_b, ln2_g, ln2_b, w_rg, b_rg, w_re, b_re, w_e_gate, w_e_up, w_e_down)
        xp_f, xp_bf, *rest_p = _prompt_layer(p, xp_f, xp_bf, bsz, t_len)
        xs_f, xs_bf, *rest_s = _sample_layer(p, l, xs_f, xs_bf, cache_k, cache_v, state_conv[l],
                                             state_gla[l], page_table, head_ind)
        outs_p.append(rest_p)
        outs_s.append(rest_s)
    stack = lambda outs, i: jnp.stack([o[i] for o in outs])
    return (xp_f.reshape(bsz, t_len, D_MODEL), xs_f.reshape(dec_b, 1, D_MODEL),
            stack(outs_p, 0), stack(outs_p, 1), stack(outs_p, 2), stack(outs_p, 3).astype(state_gla.dtype),
            stack(outs_s, 0), stack(outs_s, 1), stack(outs_s, 2), stack(outs_s, 3).astype(state_gla.dtype))
```

```python
import functools

import numpy as np
import jax
import jax.numpy as jnp
from jax import lax
from jax.experimental import pallas as pl
from jax.experimental.pallas import tpu as pltpu

D_MODEL = 1024
DEPTH = 4
H_A = 8
DH_A = 64
W_A = H_A * DH_A
MOBA_BLOCK = 256
MOBA_TOPK = 3
CONV_W = 3
W_B = 512
H_C = 4
DK_C = 64
DV_C = 128
QK_C = H_C * DK_C
V_C = H_C * DV_C
FG_RANK = 16
GATE_TEMP = 16.0
N_BRANCH = 3
N_GROUPS = 4
EXPERTS_PER_GROUP = 4
N_EXPERTS = N_GROUPS * EXPERTS_PER_GROUP
D_EXPERT = 512
ALPHA = (2 * DEPTH) ** 0.25
LN_EPS = 1e-5
NEG_INF = -1e30
LOG2E = 1.4426950408889634

LANES = 128
SUBLANES = 8
MIB = 1024 * 1024

GLA_CHUNK = 64
GLA_LEVELS = 6
GLA_TB = 512

_P_Q, _P_K, _P_V = 0, W_A, 2 * W_A
_P_B, _P_C, _P_H = 3 * W_A, 3 * W_A + W_B, 3 * W_A + 2 * W_B
_P_QC = 3 * W_A + 3 * W_B
_P_KC = _P_QC + QK_C
_P_VC = _P_KC + QK_C
_P_RC = _P_VC + V_C
_P_FL = _P_RC + V_C
_P_TOT = _P_FL + LANES

bf16 = jnp.bfloat16
f32 = jnp.float32


def _cparams(sem, vmem_mib):
    return pltpu.CompilerParams(dimension_semantics=sem, vmem_limit_bytes=vmem_mib * MIB)


def _dot(a, b):
    return jnp.dot(a, b, preferred_element_type=f32)


def _dot_nt(a, b):
    return lax.dot_general(a, b, (((1,), (1,)), ((), ())), preferred_element_type=f32)


def _layer_norm_rows(x, g, b):
    mu = jnp.mean(x, axis=-1, keepdims=True)
    xc = x - mu
    var = jnp.mean(xc * xc, axis=-1, keepdims=True)
    return xc * lax.rsqrt(var + LN_EPS) * g + b


def _silu(x):
    return x * jax.nn.sigmoid(x)


def _proj_kernel(seq_tiles, *refs):
    if seq_tiles:
        (x_ref, w_ref, wfg_ref, bfg_ref, cw_ref,
         q_ref, k_ref, v_ref, yb_ref, u_ref, qk_ref, vc_ref, rc_ref, la_ref,
         kb_ref, vb_ref, km_ref, carry_ref) = refs
    else:
        (x_ref, w_ref, wfg_ref, bfg_ref, cw_ref, p0_ref, p1_ref,
         q_ref, k_ref, v_ref, yb_ref, u_ref, qk_ref, vc_ref, rc_ref, la_ref) = refs
    x = x_ref[...]
    tm = x.shape[0]

    def cols(a, n):
        return _dot(x, w_ref[:, a:a + n])

    q = cols(_P_Q, W_A) * (DH_A ** -0.5 * LOG2E)
    k = cols(_P_K, W_A)
    v = cols(_P_V, W_A)
    k_ref[...] = k
    v_ref[...] = v
    if seq_tiles:
        q_ref[0] = jnp.transpose(q).astype(bf16)
        vb_ref[0] = jnp.transpose(v).astype(bf16)
        kb_ref[...] = k.astype(bf16)
        for j in range(tm // MOBA_BLOCK):
            km_ref[0, j:j + 1, :] = jnp.mean(k[j * MOBA_BLOCK:(j + 1) * MOBA_BLOCK], axis=0, keepdims=True)
    else:
        q_ref[...] = q.astype(bf16)

    u = cols(_P_C, W_B) * cols(_P_H, W_B)
    cw = cw_ref[...]
    if seq_tiles:
        i = pl.program_id(0)

        @pl.when(i % seq_tiles == 0)
        def _():
            carry_ref[...] = jnp.zeros_like(carry_ref)

        c6 = carry_ref[SUBLANES - 2:SUBLANES - 1, :]
        c7 = carry_ref[SUBLANES - 1:SUBLANES, :]
        row = lax.broadcasted_iota(jnp.int32, u.shape, 0)
        u1 = jnp.where(row >= 1, pltpu.roll(u, 1, 0), c7)
        u2 = jnp.where(row >= 2, pltpu.roll(u, 2, 0), jnp.where(row == 1, c7, c6))
        tail = u[tm - SUBLANES:, :]
        carry_ref[...] = tail
        u_ref[...] = tail
    else:
        u2 = p0_ref[...]
        u1 = p1_ref[...]
        u_ref[...] = u
    conv = cw[0:1, :] * u2 + cw[1:2, :] * u1 + cw[2:3, :] * u
    yb_ref[...] = (cols(_P_B, W_B) * conv).astype(yb_ref.dtype)

    qk_ref[:, 0:QK_C] = cols(_P_QC, QK_C) * (DK_C ** -0.5)
    qk_ref[:, QK_C:2 * QK_C] = cols(_P_KC, QK_C)
    vc_ref[...] = cols(_P_VC, V_C)
    rc_ref[...] = cols(_P_RC, V_C)
    fl = cols(_P_FL, LANES)
    f = _dot(fl.astype(bf16), wfg_ref[...]) + bfg_ref[...]
    la_ref[...] = (jnp.minimum(f, 0.0) - jnp.log1p(jnp.exp(-jnp.abs(f)))) * (1.0 / GATE_TEMP)


def _proj(x_bf, w_proj, wfg, bfg, conv_w, tm, seq_tiles, prev=None):
    m = x_bf.shape[0]
    n_tiles = m // tm
    row = lambda n: pl.BlockSpec((tm, n), lambda i: (i, 0))
    full = lambda a: pl.BlockSpec(a.shape, lambda i: (0,) * a.ndim)
    in_specs = [row(D_MODEL), full(w_proj), full(wfg), full(bfg), full(conv_w)]
    args = [x_bf, w_proj, wfg, bfg, conv_w]
    scratch = []
    if seq_tiles:
        u_shape, u_spec = (n_tiles * SUBLANES, W_B), pl.BlockSpec((SUBLANES, W_B), lambda i: (i, 0))
        scratch = [pltpu.VMEM((SUBLANES, W_B), f32)]
    else:
        u_shape, u_spec = (m, W_B), row(W_B)
        in_specs += [row(W_B), row(W_B)]
        args += [prev[0], prev[1]]
    tile_t = pl.BlockSpec((1, W_A, tm), lambda i: (i, 0, 0))
    out_shape = [
        jax.ShapeDtypeStruct((n_tiles, W_A, tm) if seq_tiles else (m, W_A), bf16),
        jax.ShapeDtypeStruct((m, W_A), f32),
        jax.ShapeDtypeStruct((m, W_A), f32),
        jax.ShapeDtypeStruct((m, W_B), bf16),
        jax.ShapeDtypeStruct(u_shape, f32),
        jax.ShapeDtypeStruct((m, 2 * QK_C), f32),
        jax.ShapeDtypeStruct((m, V_C), f32),
        jax.ShapeDtypeStruct((m, V_C), f32),
        jax.ShapeDtypeStruct((m, QK_C), f32),
    ]
    out_specs = [tile_t if seq_tiles else row(W_A), row(W_A), row(W_A), row(W_B), u_spec,
                 row(2 * QK_C), row(V_C), row(V_C), row(QK_C)]
    if seq_tiles:
        blocks = tm // MOBA_BLOCK
        out_shape += [jax.ShapeDtypeStruct((m, W_A), bf16),
                      jax.ShapeDtypeStruct((n_tiles, W_A, tm), bf16),
                      jax.ShapeDtypeStruct((n_tiles, blocks, W_A), f32)]
        out_specs += [row(W_A), tile_t, pl.BlockSpec((1, blocks, W_A), lambda i: (i, 0, 0))]
    return pl.pallas_call(
        functools.partial(_proj_kernel, seq_tiles),
        grid=(n_tiles,),
        in_specs=in_specs,
        out_specs=out_specs,
        out_shape=out_shape,
        scratch_shapes=scratch,
        compiler_params=_cparams(("arbitrary",), 56),
        name="proj",
    )(*args)


def _attn_prompt_kernel(qt_ref, k_ref, vt_ref, kmean_ref, o_ref, sel_ref, s_ref):
    qi = pl.program_id(2)
    nb = kmean_ref.shape[1]
    bl = MOBA_BLOCK
    n_pairs = (qi + 1) // 2
    own_slot = s_ref.shape[1] - 1

    qt = qt_ref[0]
    feat = lax.broadcasted_iota(jnp.int32, qt.shape, 0)
    head_lo = feat < DH_A
    qth = (jnp.where(head_lo, qt, jnp.zeros_like(qt)), jnp.where(head_lo, jnp.zeros_like(qt), qt))
    km = kmean_ref[0].astype(bf16)
    blk = lax.broadcasted_iota(jnp.int32, (nb, bl), 0)
    n_valid = jnp.minimum(qi, MOBA_TOPK)

    key = lax.broadcasted_iota(jnp.int32, (bl, bl), 0)
    qry = lax.broadcasted_iota(jnp.int32, (bl, bl), 1)
    own0 = pl.multiple_of(qi * bl, bl)
    k_own = k_ref[pl.ds(own0, bl), :]
    vt_pair = vt_ref[qi // 2]
    vt_own = jnp.where(qi % 2 == 0, vt_pair[:, 0:bl], vt_pair[:, bl:2 * bl])
    m_own = []
    for hh in range(2):
        s = jnp.where(blk < qi, _dot(km, qth[hh]), NEG_INF)
        chosen = jnp.zeros(s.shape, f32)
        for r in range(MOBA_TOPK):
            mx = jnp.max(s, axis=0, keepdims=True)
            first = jnp.min(jnp.where(s == mx, blk, nb), axis=0, keepdims=True)
            pick = blk == first
            chosen = jnp.where(pick, jnp.where(r < n_valid, 1.0, 0.0), chosen)
            s = jnp.where(pick, -jnp.inf, s)
        sel_ref[hh] = chosen
        so = jnp.where(key <= qry, _dot(k_own, qth[hh]), NEG_INF)
        s_ref[hh, own_slot, 0:bl, :] = so
        m_own.append(jnp.max(so, axis=0, keepdims=True))

    def scores(j2, ms):
        j0 = pl.multiple_of(j2 * 2 * bl, 2 * bl)
        kj = k_ref[pl.ds(j0, 2 * bl), :]
        out = []
        for hh in range(2):
            s = _dot(kj, qth[hh])
            m = ms[hh]
            for t in range(2):
                picked = sel_ref[hh, pl.ds(2 * j2 + t, 1), :] > 0.0
                st = jnp.where(picked, s[t * bl:(t + 1) * bl], NEG_INF)
                s_ref[hh, j2, t * bl:(t + 1) * bl, :] = st
                m = jnp.maximum(m, jnp.max(st, axis=0, keepdims=True))
            out.append(m)
        return tuple(out)

    ms = lax.fori_loop(0, n_pairs, scores, tuple(m_own))

    l0, acc0 = [], []
    for hh in range(2):
        p = jnp.exp2(s_ref[hh, own_slot, 0:bl, :] - ms[hh])
        l0.append(jnp.sum(p, axis=0, keepdims=True))
        acc0.append(_dot(vt_own, p.astype(bf16)))

    def values(j2, carry):
        vtj = vt_ref[j2]
        ls, accs = carry
        new_l, new_acc = [], []
        for hh in range(2):
            p = jnp.exp2(s_ref[hh, j2] - ms[hh])
            new_l.append(ls[hh] + jnp.sum(p, axis=0, keepdims=True))
            new_acc.append(accs[hh] + _dot(vtj, p.astype(bf16)))
        return tuple(new_l), tuple(new_acc)

    ls, accs = lax.fori_loop(0, n_pairs, values, (tuple(l0), tuple(acc0)))
    out_t = jnp.where(head_lo, accs[0] / ls[0], accs[1] / ls[1])
    o_ref[...] = jnp.transpose(out_t).astype(o_ref.dtype)


def _attn_prompt(qt, k, vt, kmean, bsz, t_len):
    nb = t_len // MOBA_BLOCK
    bl = MOBA_BLOCK
    pair = 2 * DH_A
    tile = qt.shape[2]
    assert tile == 2 * bl and t_len % tile == 0
    n_tiles = t_len // tile
    slots = n_tiles + 1
    return pl.pallas_call(
        _attn_prompt_kernel,
        grid=(bsz, W_A // pair, nb),
        in_specs=[pl.BlockSpec((1, pair, bl), lambda b, hp, qi: (b * n_tiles + qi // 2, hp, qi % 2)),
                  pl.BlockSpec((t_len, pair), lambda b, hp, qi: (b, hp)),
                  pl.BlockSpec((n_tiles, pair, tile), lambda b, hp, qi: (b, hp, 0)),
                  pl.BlockSpec((1, nb, pair), lambda b, hp, qi: (b, 0, hp))],
        out_specs=pl.BlockSpec((bl, pair), lambda b, hp, qi: (b * nb + qi, hp)),
        out_shape=jax.ShapeDtypeStruct((bsz * t_len, W_A), bf16),
        scratch_shapes=[
            pltpu.VMEM((2, nb, bl), f32),
            pltpu.VMEM((2, slots, tile, bl), f32),
        ],
        compiler_params=_cparams(("arbitrary", "arbitrary", "arbitrary"), 48),
        name="attn_prompt",
    )(qt, k, vt, kmean)


def _attn_select_kernel(layer, pt_ref, q_ref, cache_ref, idx_ref, kbuf_ref, sem_ref):
    b = pl.program_id(0)
    n_pages, page = kbuf_ref.shape[0], kbuf_ref.shape[3]
    per_block = MOBA_BLOCK // page
    nbk = n_pages // per_block

    def page_copy(p):
        return pltpu.make_async_copy(cache_ref.at[layer, pt_ref[b, p]], kbuf_ref.at[p], sem_ref.at[0])

    for p in range(n_pages):
        page_copy(p).start()
    for p in range(n_pages):
        page_copy(p).wait()

    qb = jnp.broadcast_to(q_ref[0], (H_A, DH_A, page))
    scores = []
    for j in range(nbk):
        tok = jnp.sum(kbuf_ref[j * per_block] * qb, axis=1)
        for t in range(1, per_block):
            tok = tok + jnp.sum(kbuf_ref[j * per_block + t] * qb, axis=1)
        scores.append(jnp.sum(tok, axis=1, keepdims=True) * (1.0 / MOBA_BLOCK))

    idx_ref[0] = jnp.zeros(idx_ref.shape[1:], jnp.int32)
    for r in range(MOBA_TOPK):
        mx = functools.reduce(jnp.maximum, scores)
        first = jnp.full(mx.shape, nbk, jnp.int32)
        for j in reversed(range(nbk)):
            first = jnp.where(scores[j] == mx, j, first)
        idx_ref[0, :, r:r + 1] = first
        scores = [jnp.where(first == j, -jnp.inf, s) for j, s in enumerate(scores)]


def _attn_select(page_table, q_col, cache_kt, layer):
    bsz, n_pages = page_table.shape
    page = cache_kt.shape[4]
    grid_spec = pltpu.PrefetchScalarGridSpec(
        num_scalar_prefetch=1,
        grid=(bsz,),
        in_specs=[
            pl.BlockSpec((1, H_A, DH_A, 1), lambda b, pt: (b, 0, 0, 0)),
            pl.BlockSpec(memory_space=pl.ANY),
        ],
        out_specs=pl.BlockSpec((1, H_A, LANES), lambda b, pt: (b, 0, 0)),
        scratch_shapes=[pltpu.VMEM((n_pages, H_A, DH_A, page), f32), pltpu.SemaphoreType.DMA((1,))],
    )
    return pl.pallas_call(
        functools.partial(_attn_select_kernel, layer),
        grid_spec=grid_spec,
        out_shape=jax.ShapeDtypeStruct((bsz, H_A, LANES), jnp.int32),
        compiler_params=_cparams(("arbitrary",), 40),
        name="attn_select",
    )(page_table, q_col, cache_kt)


def _attn_sample_kernel(layer, pt_ref, sel_ref, q_ref, kn_ref, vn_ref, ck_ref, cv_ref, o_ref,
                        kbuf_ref, vbuf_ref, sem_ref):
    b = pl.program_id(0)
    page = ck_ref.shape[4]
    per_block = MOBA_BLOCK // page

    def copies(h, r, t):
        pg = pt_ref[b, sel_ref[b, h * MOBA_TOPK + r] * per_block + t]
        dst = pl.ds((r * per_block + t) * page, page)
        return (pltpu.make_async_copy(ck_ref.at[layer, pg, h], kbuf_ref.at[h, :, dst], sem_ref.at[0]),
                pltpu.make_async_copy(cv_ref.at[layer, pg, h], vbuf_ref.at[h, :, dst], sem_ref.at[1]))

    slots = [(h, r, t) for h in range(H_A) for r in range(MOBA_TOPK) for t in range(per_block)]
    for s in slots:
        ck, cv = copies(*s)
        ck.start()
        cv.start()
    for s in slots:
        ck, cv = copies(*s)
        ck.wait()
        cv.wait()

    for h in range(H_A):
        qh = q_ref[0, h:h + 1, :]
        kh = kbuf_ref[h].astype(bf16)
        vh = vbuf_ref[h].astype(bf16)
        q8 = jnp.broadcast_to(qh, (SUBLANES, DH_A)).astype(bf16)
        s = _dot(q8, kh)
        s_own = jnp.sum(qh * kn_ref[0, h:h + 1, :], axis=1, keepdims=True)
        m = jnp.maximum(jnp.max(s, axis=1, keepdims=True), s_own)
        p = jnp.exp2(s - m)
        p_own = jnp.exp2(s_own - m)
        den = jnp.sum(p, axis=1, keepdims=True) + p_own
        out = (_dot_nt(p.astype(bf16), vh) + p_own * vn_ref[0, h:h + 1, :]) / den
        o_ref[0, h:h + 1, :] = out[0:1, :].astype(o_ref.dtype)


def _attn_sample(page_table, sel, q, k_new, v_new, cache_k, cache_v, layer):
    bsz = page_table.shape[0]
    n_sel = MOBA_TOPK * MOBA_BLOCK
    tok = pl.BlockSpec((1, H_A, DH_A), lambda b, pt, s: (b, 0, 0))
    grid_spec = pltpu.PrefetchScalarGridSpec(
        num_scalar_prefetch=2,
        grid=(bsz,),
        in_specs=[tok, tok, tok, pl.BlockSpec(memory_space=pl.ANY), pl.BlockSpec(memory_space=pl.ANY)],
        out_specs=tok,
        scratch_shapes=[pltpu.VMEM((H_A, DH_A, n_sel), f32), pltpu.VMEM((H_A, DH_A, n_sel), f32),
                        pltpu.SemaphoreType.DMA((2,))],
    )
    return pl.pallas_call(
        functools.partial(_attn_sample_kernel, layer),
        grid_spec=grid_spec,
        out_shape=jax.ShapeDtypeStruct((bsz, H_A, DH_A), bf16),
        compiler_params=_cparams(("arbitrary",), 32),
        name="attn_sample",
    )(page_table, sel, q, k_new, v_new, cache_k, cache_v)


def _gla_tables():
    c = GLA_CHUNK
    tri = np.tril(np.ones((c, c), np.float32))
    mats = [tri]
    masks = []
    t = np.arange(c)
    for lev in range(GLA_LEVELS):
        h = 1 << lev
        mid = (t // (2 * h)) * (2 * h) + h
        mats.append(tri[mid - 1])
        same = (t[:, None] // (2 * h)) == (t[None, :] // (2 * h))
        masks.append(same & ((t[:, None] & h) != 0) & ((t[None, :] & h) == 0))
    masks.append(np.eye(c, dtype=bool))
    sel = np.concatenate(mats, axis=0)
    lmask = np.stack([np.tile(mk, (H_C, 1)) for mk in masks]).astype(np.float32)
    return sel, lmask


def _gla_prompt_kernel(qk_ref, la_ref, v_ref, r_ref, g_ref, s0_ref, sel_ref, lmask_ref,
                       y_ref, sout_ref, s_ref):
    c = GLA_CHUNK

    @pl.when(pl.program_id(1) == 0)
    def _():
        s_ref[...] = s0_ref[0]

    sel = sel_ref[...]
    g = g_ref[...]
    lane = lax.broadcasted_iota(jnp.int32, (c, QK_C), 1)
    rowc = lax.broadcasted_iota(jnp.int32, (c, QK_C), 0)
    head_of_lane = lane // DK_C
    head_of_row = lax.broadcasted_iota(jnp.int32, (QK_C, DV_C), 0) // DK_C

    def stack_heads(a):
        return jnp.concatenate([jnp.where(head_of_lane == h, a, 0.0) for h in range(H_C)], axis=0)

    def chunk(ci, carry):
        r0 = pl.multiple_of(ci * c, c)
        la = la_ref[pl.ds(r0, c), :]
        la1 = la.astype(bf16)
        rem = la - la1.astype(f32)
        la2 = rem.astype(bf16)
        la3 = (rem - la2.astype(f32)).astype(bf16)
        ball = _dot(sel, la1) + _dot(sel, la2) + _dot(sel, la3)
        b = ball[0:c]
        q = qk_ref[pl.ds(r0, c), 0:QK_C]
        k = qk_ref[pl.ds(r0, c), QK_C:2 * QK_C]
        v = v_ref[pl.ds(r0, c), :].astype(bf16)

        att = lmask_ref[GLA_LEVELS] * _dot_nt(stack_heads(q).astype(bf16), k.astype(bf16))
        for lev in range(GLA_LEVELS):
            ref_b = ball[(lev + 1) * c:(lev + 2) * c]
            right = (rowc & (1 << lev)) != 0
            qt = q * jnp.exp(jnp.where(right, b - ref_b, NEG_INF))
            kt = k * jnp.exp(jnp.where(right, NEG_INF, ref_b - b))
            att = att + lmask_ref[lev] * _dot_nt(stack_heads(qt).astype(bf16), kt.astype(bf16))

        s_old = s_ref[...]
        inter = _dot(stack_heads(q * jnp.exp(b)).astype(bf16), s_old.astype(bf16))
        b_last = b[c - 1:c, :]
        kt_t = jnp.transpose(k * jnp.exp(b_last - b))
        b_t = jnp.transpose(b)
        upd = _dot(kt_t.astype(bf16), v)
        s_new = jnp.exp(b_t[:, c - 1:c]) * s_old
        for h in range(H_C):
            s_new = s_new + jnp.where(head_of_row == h, upd[:, h * DV_C:(h + 1) * DV_C], 0.0)
        s_ref[...] = s_new

        att_bf = att.astype(bf16)
        for h in range(H_C):
            o = (_dot(att_bf[h * c:(h + 1) * c], v[:, h * DV_C:(h + 1) * DV_C])
                 + inter[h * c:(h + 1) * c])
            mu = jnp.mean(o, axis=-1, keepdims=True)
            oc = o - mu
            var = jnp.mean(oc * oc, axis=-1, keepdims=True)
            hs = slice(h * DV_C, (h + 1) * DV_C)
            gate = g[:, hs] * _silu(r_ref[pl.ds(r0, c), hs])
            y_ref[pl.ds(r0, c), hs] = (oc * lax.rsqrt(var + LN_EPS) * gate).astype(y_ref.dtype)
        return carry

    lax.fori_loop(0, qk_ref.shape[0] // c, chunk, 0)
    sout_ref[0] = s_ref[...]


def _gla_prompt(qk, la, v, r, g_gla, s0, bsz, t_len):
    tb = min(GLA_TB, t_len)
    nt = t_len // tb
    sel_np, lmask_np = _gla_tables()
    sel = jnp.asarray(sel_np, bf16)
    lmask = jnp.asarray(lmask_np, f32)
    row = lambda n: pl.BlockSpec((tb, n), lambda b, t: (b * nt + t, 0))
    st = pl.BlockSpec((1, QK_C, DV_C), lambda b, t: (b, 0, 0))
    return pl.pallas_call(
        _gla_prompt_kernel,
        grid=(bsz, nt),
        in_specs=[row(2 * QK_C), row(QK_C), row(V_C), row(V_C),
                  pl.BlockSpec((1, V_C), lambda b, t: (0, 0)), st,
                  pl.BlockSpec(sel.shape, lambda b, t: (0, 0)),
                  pl.BlockSpec(lmask.shape, lambda b, t: (0, 0, 0))],
        out_specs=[row(V_C), st],
        out_shape=[jax.ShapeDtypeStruct((bsz * t_len, V_C), bf16),
                   jax.ShapeDtypeStruct((bsz, QK_C, DV_C), f32)],
        scratch_shapes=[pltpu.VMEM((QK_C, DV_C), f32)],
        compiler_params=_cparams(("arbitrary", "arbitrary"), 40),
        name="gla_prompt",
    )(qk, la, v, r, g_gla, s0, sel, lmask)


def _gla_step_kernel(q_ref, k_ref, la_ref, v_ref, r_ref, g_ref, s_ref, y_ref, sout_ref):
    head_of_row = lax.broadcasted_iota(jnp.int32, (QK_C, DV_C), 0) // DK_C
    v = v_ref[0]
    v_rows = jnp.zeros((QK_C, DV_C), f32)
    for h in range(H_C):
        v_rows = jnp.where(head_of_row == h, v[:, h * DV_C:(h + 1) * DV_C], v_rows)
    s_new = jnp.exp(la_ref[0]) * s_ref[0] + k_ref[0] * v_rows
    sout_ref[0] = s_new
    qs = q_ref[0] * s_new
    g = g_ref[...]
    for h in range(H_C):
        o = jnp.sum(qs[h * DK_C:(h + 1) * DK_C], axis=0, keepdims=True)
        mu = jnp.mean(o, axis=-1, keepdims=True)
        oc = o - mu
        var = jnp.mean(oc * oc, axis=-1, keepdims=True)
        hs = slice(h * DV_C, (h + 1) * DV_C)
        y_ref[0, :, hs] = (oc * lax.rsqrt(var + LN_EPS) * g[:, hs] * _silu(r_ref[0, :, hs])).astype(y_ref.dtype)


def _gla_step(q_col, k_col, la_col, v, r, g_gla, s):
    bsz = s.shape[0]
    col = pl.BlockSpec((1, QK_C, 1), lambda b: (b, 0, 0))
    tok = pl.BlockSpec((1, 1, V_C), lambda b: (b, 0, 0))
    st = pl.BlockSpec((1, QK_C, DV_C), lambda b: (b, 0, 0))
    return pl.pallas_call(
        _gla_step_kernel,
        grid=(bsz,),
        in_specs=[col, col, col, tok, tok, pl.BlockSpec((1, V_C), lambda b: (0, 0)), st],
        out_specs=[tok, st],
        out_shape=[jax.ShapeDtypeStruct((bsz, 1, V_C), bf16), jax.ShapeDtypeStruct(s.shape, f32)],
        compiler_params=_cparams(("arbitrary",), 16),
        name="gla_step",
    )(q_col, k_col, la_col, v, r, g_gla, s)


def _merge_kernel(xb_ref, xf_ref, ya_ref, yb_ref, yc_ref, wg_ref, wa_ref, wb_ref, wc_ref, wo_ref,
                  g_ref, b_ref, hf_ref, hb_ref):
    xb = xb_ref[...]
    merged = None
    for i, (y_ref, w_ref) in enumerate(((ya_ref, wa_ref), (yb_ref, wb_ref), (yc_ref, wc_ref))):
        gate = jax.nn.sigmoid(_dot(xb, wg_ref[:, i * D_MODEL:(i + 1) * D_MODEL]))
        term = gate * _dot(y_ref[...], w_ref[...])
        merged = term if merged is None else merged + term
    mix = _dot(merged.astype(bf16), wo_ref[...])
    h = _layer_norm_rows(ALPHA * xf_ref[...] + mix, g_ref[...], b_ref[...])
    hf_ref[...] = h
    hb_ref[...] = h.astype(bf16)


def _merge(x_bf, x_f, y_a, y_b, y_c, w_gate, w_a, w_b, w_c, w_o, ln_g, ln_b, tm):
    m = x_bf.shape[0]
    row = lambda n: pl.BlockSpec((tm, n), lambda i: (i, 0))
    full = lambda a: pl.BlockSpec(a.shape, lambda i: (0,) * a.ndim)
    return pl.pallas_call(
        _merge_kernel,
        grid=(m // tm,),
        in_specs=[row(D_MODEL), row(D_MODEL), row(W_A), row(W_B), row(V_C),
                  full(w_gate), full(w_a), full(w_b), full(w_c), full(w_o), full(ln_g), full(ln_b)],
        out_specs=[row(D_MODEL), row(D_MODEL)],
        out_shape=[jax.ShapeDtypeStruct((m, D_MODEL), f32), jax.ShapeDtypeStruct((m, D_MODEL), bf16)],
        compiler_params=_cparams(("arbitrary",), 56),
        name="merge",
    )(x_bf, x_f, y_a, y_b, y_c, w_gate, w_a, w_b, w_c, w_o, ln_g, ln_b)


def _moe_kernel(hb_ref, hf_ref, wr_ref, br_ref, wg_ref, wu_ref, wd_ref, g_ref, b_ref,
                yf_ref, yb_ref, comb_ref, acc_ref):
    e = pl.program_id(1)
    hb = hb_ref[...]

    @pl.when(e == 0)
    def _():
        logits = _dot(hb, wr_ref[...]) + br_ref[...]
        lane = lax.broadcasted_iota(jnp.int32, logits.shape, 1)
        gl = jnp.where(lane < N_GROUPS, logits, -jnp.inf)
        gmax = jnp.max(gl, axis=1, keepdims=True)
        pg = jnp.exp(gl - gmax)
        pg = pg / jnp.sum(pg, axis=1, keepdims=True)
        p_top = jnp.max(pg, axis=1, keepdims=True)
        g_idx = jnp.min(jnp.where(pg == p_top, lane, LANES), axis=1, keepdims=True)
        e0 = N_GROUPS + g_idx * EXPERTS_PER_GROUP
        el = jnp.where((lane >= e0) & (lane < e0 + EXPERTS_PER_GROUP), logits, -jnp.inf)
        pe = jnp.exp(el - jnp.max(el, axis=1, keepdims=True))
        pe = pe / jnp.sum(pe, axis=1, keepdims=True)
        p1 = jnp.max(pe, axis=1, keepdims=True)
        i1 = jnp.min(jnp.where(pe == p1, lane, LANES), axis=1, keepdims=True)
        rest = jnp.where(lane == i1, -1.0, pe)
        p2 = jnp.max(rest, axis=1, keepdims=True)
        i2 = jnp.min(jnp.where(rest == p2, lane, LANES), axis=1, keepdims=True)
        den = p1 + p2
        comb_ref[...] = jnp.where(lane == i1, p_top * p1 / den, jnp.where(lane == i2, p_top * p2 / den, 0.0))
        acc_ref[...] = jnp.zeros_like(acc_ref)

    comb = comb_ref[...]
    lane = lax.broadcasted_iota(jnp.int32, comb.shape, 1)
    cw = jnp.sum(jnp.where(lane == N_GROUPS + e, comb, 0.0), axis=1, keepdims=True)
    act = _silu(_dot(hb, wg_ref[0])) * _dot(hb, wu_ref[0]) * cw
    acc_ref[...] += _dot(act.astype(bf16), wd_ref[0])

    @pl.when(e == N_EXPERTS - 1)
    def _():
        y = _layer_norm_rows(ALPHA * hf_ref[...] + acc_ref[...], g_ref[...], b_ref[...])
        yf_ref[...] = y
        yb_ref[...] = y.astype(bf16)


def _moe(h_bf, h_f, w_router, b_router, w_gate, w_up, w_down, ln_g, ln_b, tm):
    m = h_bf.shape[0]
    row = lambda n: pl.BlockSpec((tm, n), lambda i, e: (i, 0))
    full = lambda a: pl.BlockSpec(a.shape, lambda i, e: (0,) * a.ndim)
    return pl.pallas_call(
        _moe_kernel,
        grid=(m // tm, N_EXPERTS),
        in_specs=[row(D_MODEL), row(D_MODEL), full(w_router), full(b_router),
                  pl.BlockSpec((1, D_MODEL, D_EXPERT), lambda i, e: (e, 0, 0)),
                  pl.BlockSpec((1, D_MODEL, D_EXPERT), lambda i, e: (e, 0, 0)),
                  pl.BlockSpec((1, D_EXPERT, D_MODEL), lambda i, e: (e, 0, 0)),
                  full(ln_g), full(ln_b)],
        out_specs=[row(D_MODEL), row(D_MODEL)],
        out_shape=[jax.ShapeDtypeStruct((m, D_MODEL), f32), jax.ShapeDtypeStruct((m, D_MODEL), bf16)],
        scratch_shapes=[pltpu.VMEM((tm, LANES), f32), pltpu.VMEM((tm, D_MODEL), f32)],
        compiler_params=_cparams(("arbitrary", "arbitrary"), 48),
        name="moe",
    )(h_bf, h_f, w_router, b_router, w_gate, w_up, w_down, ln_g, ln_b)


def _row_tile(m):
    for tm in (512, 256, 128, 64, 32, 16, 8):
        if m % tm == 0:
            return tm
    raise ValueError(f"row count {m} must be a multiple of {SUBLANES}")


def _layer_weights(l, w_in, conv_w, w_fg2, b_fg, g_gla, w_br_a, w_br_b, w_br_c, w_out,
                   ln1_g, ln1_b, ln2_g, ln2_b, w_rg, b_rg, w_re, b_re, w_e_gate, w_e_up, w_e_down):
    wi = w_in[l]
    n_front = _P_FL
    gate0 = n_front + FG_RANK
    w_proj = jnp.concatenate(
        [wi[:, :n_front], jnp.pad(wi[:, n_front:gate0], ((0, 0), (0, LANES - FG_RANK)))], axis=1).astype(bf16)
    pad_r = LANES - N_GROUPS - N_EXPERTS
    return dict(
        w_proj=w_proj,
        w_gate=wi[:, gate0:].astype(bf16),
        wfg=jnp.pad(w_fg2[l], ((0, LANES - FG_RANK), (0, 0))).astype(bf16),
        bfg=b_fg[l][None, :],
        conv_w=conv_w[l],
        g_gla=g_gla[l][None, :],
        w_a=w_br_a[l].astype(bf16), w_b=w_br_b[l].astype(bf16), w_c=w_br_c[l].astype(bf16),
        w_o=w_out[l].astype(bf16),
        ln1_g=ln1_g[l][None, :], ln1_b=ln1_b[l][None, :], ln2_g=ln2_g[l][None, :], ln2_b=ln2_b[l][None, :],
        w_router=jnp.pad(jnp.concatenate([w_rg[l], w_re[l]], axis=1), ((0, 0), (0, pad_r))).astype(bf16),
        b_router=jnp.pad(jnp.concatenate([b_rg[l], b_re[l]]), (0, pad_r))[None, :],
        w_e_gate=w_e_gate[l].astype(bf16), w_e_up=w_e_up[l].astype(bf16), w_e_down=w_e_down[l].astype(bf16),
    )


def _finish_layer(p, x_bf, x_f, y_a, y_b, y_c, tm):
    h_f, h_bf = _merge(x_bf, x_f, y_a, y_b, y_c, p["w_gate"], p["w_a"], p["w_b"], p["w_c"], p["w_o"],
                       p["ln1_g"], p["ln1_b"], tm)
    return _moe(h_bf, h_f, p["w_router"], p["b_router"], p["w_e_gate"], p["w_e_up"], p["w_e_down"],
                p["ln2_g"], p["ln2_b"], tm)


def _prompt_layer(p, x_f, x_bf, bsz, t_len):
    tm = _row_tile(t_len)
    q, k, v, y_b, u_tail, qk, vc, rc, la, k_bf, v_bf, kmean = _proj(
        x_bf, p["w_proj"], p["wfg"], p["bfg"], p["conv_w"], tm, t_len // tm)
    y_a = _attn_prompt(q, k_bf, v_bf, kmean.reshape(bsz, t_len // MOBA_BLOCK, W_A), bsz, t_len)
    y_c, s_fin = _gla_prompt(qk, la, vc, rc, p["g_gla"], jnp.zeros((bsz, QK_C, DV_C), f32), bsz, t_len)
    x_f, x_bf = _finish_layer(p, x_bf, x_f, y_a, y_b, y_c, tm)
    conv_state = u_tail.reshape(bsz, t_len // tm, SUBLANES, W_B)[:, -1, SUBLANES - (CONV_W - 1):, :]
    return (x_f, x_bf, k.reshape(bsz, t_len, H_A, DH_A), v.reshape(bsz, t_len, H_A, DH_A), conv_state,
            s_fin.reshape(bsz, H_C, DK_C, DV_C))


def _sample_layer(p, layer, x_f, x_bf, cache_kt, cache_vt, conv_prev, gla_prev, page_table):
    bsz = x_f.shape[0]
    q, k, v, y_b, u, qk, vc, rc, la = _proj(
        x_bf, p["w_proj"], p["wfg"], p["bfg"], p["conv_w"], bsz, 0, prev=(conv_prev[:, 0], conv_prev[:, 1]))
    heads = lambda a: a.reshape(bsz, H_A, DH_A)
    q_h = heads(q.astype(f32))
    idx = _attn_select(page_table, q_h[..., None], cache_kt, layer)
    sel = idx[:, :, :MOBA_TOPK].reshape(bsz, H_A * MOBA_TOPK)
    y_a = _attn_sample(page_table, sel, q_h, heads(k), heads(v), cache_kt, cache_vt, layer)
    y_c, s_new = _gla_step(qk[:, :QK_C, None], qk[:, QK_C:, None], la[:, :, None], vc[:, None, :],
                           rc[:, None, :], p["g_gla"], gla_prev.reshape(bsz, QK_C, DV_C))
    x_f, x_bf = _finish_layer(p, x_bf, x_f, y_a.reshape(bsz, W_A), y_b, y_c.reshape(bsz, V_C), bsz)
    conv_state = jnp.stack([conv_prev[:, 1], u], axis=1)
    return (x_f, x_bf, k.reshape(bsz, 1, H_A, DH_A), v.reshape(bsz, 1, H_A, DH_A), conv_state,
            s_new.reshape(bsz, H_C, DK_C, DV_C))


def kernel(x_prompt, x_sample, cache_k, cache_v, state_conv, state_gla, page_table, w_in, conv_w, w_fg2, b_fg, g_gla, w_br_a, w_br_b, w_br_c, w_out, ln1_g, ln1_b, ln2_g, ln2_b, w_rg, b_rg, w_re, b_re, w_e_gate, w_e_up, w_e_down):
    bsz, t_len, _ = x_prompt.shape
    dec_b = x_sample.shape[0]
    assert x_sample.shape[1] == 1 and t_len % MOBA_BLOCK == 0 and dec_b % SUBLANES == 0
    assert (page_table.shape[1] * cache_k.shape[2]) % MOBA_BLOCK == 0 and MOBA_BLOCK % cache_k.shape[2] == 0
    cache_kt = jnp.transpose(cache_k, (0, 1, 3, 4, 2))
    cache_vt = jnp.transpose(cache_v, (0, 1, 3, 4, 2))
    xp_f = x_prompt.reshape(bsz * t_len, D_MODEL)
    xs_f = x_sample.reshape(dec_b, D_MODEL)
    xp_bf, xs_bf = xp_f.astype(bf16), xs_f.astype(bf16)
    outs_p, outs_s = [], []
    for l in range(DEPTH):
        p = _layer_weights(l, w_in, conv_w, w_fg2, b_fg, g_gla, w_br_a, w_br_b, w_br_c, w_out,
                           ln1_g, ln1_b, ln2_g, ln2_b, w_rg, b_rg, w_re, b_re, w_e_gate, w_e_up, w_e_down)
        xp_f, xp_bf, *rest_p = _prompt_layer(p, xp_f, xp_bf, bsz, t_len)
        xs_f, xs_bf, *rest_s = _sample_layer(p, l, xs_f, xs_bf, cache_kt, cache_vt, state_conv[l],
                                             state_gla[l], page_table)
        outs_p.append(rest_p)
        outs_s.append(rest_s)
    stack = lambda outs, i: jnp.stack([o[i] for o in outs])
    return (xp_f.reshape(bsz, t_len, D_MODEL), xs_f.reshape(dec_b, 1, D_MODEL),
            stack(outs_p, 0), stack(outs_p, 1), stack(outs_p, 2), stack(outs_p, 3).astype(state_gla.dtype),
            stack(outs_s, 0), stack(outs_s, 1), stack(outs_s, 2), stack(outs_s, 3).astype(state_gla.dtype))
```

```python
import functools

import numpy as np
import jax
import jax.numpy as jnp
from jax import lax
from jax.experimental import pallas as pl
from jax.experimental.pallas import tpu as pltpu

D_MODEL = 1024
DEPTH = 4
H_A = 8
DH_A = 64
W_A = H_A * DH_A
MOBA_BLOCK = 256
MOBA_TOPK = 3
CONV_W = 3
W_B = 512
H_C = 4
DK_C = 64
DV_C = 128
QK_C = H_C * DK_C
V_C = H_C * DV_C
FG_RANK = 16
GATE_TEMP = 16.0
N_BRANCH = 3
N_GROUPS = 4
EXPERTS_PER_GROUP = 4
N_EXPERTS = N_GROUPS * EXPERTS_PER_GROUP
D_EXPERT = 512
ALPHA = (2 * DEPTH) ** 0.25
LN_EPS = 1e-5
NEG_INF = -1e30
LOG2E = 1.4426950408889634

LANES = 128
SUBLANES = 8
MIB = 1024 * 1024

GLA_CHUNK = 64
GLA_LEVELS = 6
GLA_TB = 512

_P_Q, _P_K, _P_V = 0, W_A, 2 * W_A
_P_B, _P_C, _P_H = 3 * W_A, 3 * W_A + W_B, 3 * W_A + 2 * W_B
_P_QC = 3 * W_A + 3 * W_B
_P_KC = _P_QC + QK_C
_P_VC = _P_KC + QK_C
_P_RC = _P_VC + V_C
_P_FL = _P_RC + V_C
_P_TOT = _P_FL + LANES

bf16 = jnp.bfloat16
f32 = jnp.float32


def _cparams(sem, vmem_mib):
    return pltpu.CompilerParams(dimension_semantics=sem, vmem_limit_bytes=vmem_mib * MIB)


def _dot(a, b):
    return jnp.dot(a, b, preferred_element_type=f32)


def _dot_nt(a, b):
    return lax.dot_general(a, b, (((1,), (1,)), ((), ())), preferred_element_type=f32)


def _layer_norm_rows(x, g, b):
    mu = jnp.mean(x, axis=-1, keepdims=True)
    xc = x - mu
    var = jnp.mean(xc * xc, axis=-1, keepdims=True)
    return xc * lax.rsqrt(var + LN_EPS) * g + b


def _silu(x):
    return x * jax.nn.sigmoid(x)


def _proj_kernel(seq_tiles, *refs):
    if seq_tiles:
        (x_ref, w_ref, wfg_ref, bfg_ref, cw_ref,
         q_ref, k_ref, v_ref, yb_ref, u_ref, qk_ref, vc_ref, rc_ref, la_ref,
         kb_ref, vb_ref, km_ref, carry_ref) = refs
    else:
        (x_ref, w_ref, wfg_ref, bfg_ref, cw_ref, p0_ref, p1_ref,
         q_ref, k_ref, v_ref, yb_ref, u_ref, qk_ref, vc_ref, rc_ref, la_ref) = refs
    x = x_ref[...]
    tm = x.shape[0]

    def cols(a, n):
        return _dot(x, w_ref[:, a:a + n])

    q = cols(_P_Q, W_A) * (DH_A ** -0.5 * LOG2E)
    k = cols(_P_K, W_A)
    v = cols(_P_V, W_A)
    k_ref[...] = k
    v_ref[...] = v
    if seq_tiles:
        q_ref[0] = jnp.transpose(q).astype(bf16)
        vb_ref[0] = jnp.transpose(v).astype(bf16)
        kb_ref[...] = k.astype(bf16)
        for j in range(tm // MOBA_BLOCK):
            km_ref[0, j:j + 1, :] = jnp.mean(k[j * MOBA_BLOCK:(j + 1) * MOBA_BLOCK], axis=0, keepdims=True)
    else:
        q_ref[...] = q.astype(bf16)

    u = cols(_P_C, W_B) * cols(_P_H, W_B)
    cw = cw_ref[...]
    if seq_tiles:
        i = pl.program_id(0)

        @pl.when(i % seq_tiles == 0)
        def _():
            carry_ref[...] = jnp.zeros_like(carry_ref)

        c6 = carry_ref[SUBLANES - 2:SUBLANES - 1, :]
        c7 = carry_ref[SUBLANES - 1:SUBLANES, :]
        row = lax.broadcasted_iota(jnp.int32, u.shape, 0)
        u1 = jnp.where(row >= 1, pltpu.roll(u, 1, 0), c7)
        u2 = jnp.where(row >= 2, pltpu.roll(u, 2, 0), jnp.where(row == 1, c7, c6))
        tail = u[tm - SUBLANES:, :]
        carry_ref[...] = tail
        u_ref[...] = tail
    else:
        u2 = p0_ref[...]
        u1 = p1_ref[...]
        u_ref[...] = u
    conv = cw[0:1, :] * u2 + cw[1:2, :] * u1 + cw[2:3, :] * u
    yb_ref[...] = (cols(_P_B, W_B) * conv).astype(yb_ref.dtype)

    qk_ref[:, 0:QK_C] = cols(_P_QC, QK_C) * (DK_C ** -0.5)
    qk_ref[:, QK_C:2 * QK_C] = cols(_P_KC, QK_C)
    vc_ref[...] = cols(_P_VC, V_C)
    rc_ref[...] = cols(_P_RC, V_C)
    fl = cols(_P_FL, LANES)
    f = _dot(fl.astype(bf16), wfg_ref[...]) + bfg_ref[...]
    la_ref[...] = (jnp.minimum(f, 0.0) - jnp.log1p(jnp.exp(-jnp.abs(f)))) * (1.0 / GATE_TEMP)


def _proj(x_bf, w_proj, wfg, bfg, conv_w, tm, seq_tiles, prev=None):
    m = x_bf.shape[0]
    n_tiles = m // tm
    row = lambda n: pl.BlockSpec((tm, n), lambda i: (i, 0))
    full = lambda a: pl.BlockSpec(a.shape, lambda i: (0,) * a.ndim)
    in_specs = [row(D_MODEL), full(w_proj), full(wfg), full(bfg), full(conv_w)]
    args = [x_bf, w_proj, wfg, bfg, conv_w]
    scratch = []
    if seq_tiles:
        u_shape, u_spec = (n_tiles * SUBLANES, W_B), pl.BlockSpec((SUBLANES, W_B), lambda i: (i, 0))
        scratch = [pltpu.VMEM((SUBLANES, W_B), f32)]
    else:
        u_shape, u_spec = (m, W_B), row(W_B)
        in_specs += [row(W_B), row(W_B)]
        args += [prev[0], prev[1]]
    tile_t = pl.BlockSpec((1, W_A, tm), lambda i: (i, 0, 0))
    out_shape = [
        jax.ShapeDtypeStruct((n_tiles, W_A, tm) if seq_tiles else (m, W_A), bf16),
        jax.ShapeDtypeStruct((m, W_A), f32),
        jax.ShapeDtypeStruct((m, W_A), f32),
        jax.ShapeDtypeStruct((m, W_B), bf16),
        jax.ShapeDtypeStruct(u_shape, f32),
        jax.ShapeDtypeStruct((m, 2 * QK_C), f32),
        jax.ShapeDtypeStruct((m, V_C), f32),
        jax.ShapeDtypeStruct((m, V_C), f32),
        jax.ShapeDtypeStruct((m, QK_C), f32),
    ]
    out_specs = [tile_t if seq_tiles else row(W_A), row(W_A), row(W_A), row(W_B), u_spec,
                 row(2 * QK_C), row(V_C), row(V_C), row(QK_C)]
    if seq_tiles:
        blocks = tm // MOBA_BLOCK
        out_shape += [jax.ShapeDtypeStruct((m, W_A), bf16),
                      jax.ShapeDtypeStruct((n_tiles, W_A, tm), bf16),
                      jax.ShapeDtypeStruct((n_tiles, blocks, W_A), f32)]
        out_specs += [row(W_A), tile_t, pl.BlockSpec((1, blocks, W_A), lambda i: (i, 0, 0))]
    return pl.pallas_call(
        functools.partial(_proj_kernel, seq_tiles),
        grid=(n_tiles,),
        in_specs=in_specs,
        out_specs=out_specs,
        out_shape=out_shape,
        scratch_shapes=scratch,
        compiler_params=_cparams(("arbitrary",), 56),
        name="proj",
    )(*args)


def _attn_prompt_kernel(qt_ref, k_ref, vt_ref, kmean_ref, o_ref, sel_ref, s_ref):
    qi = pl.program_id(2)
    nb = kmean_ref.shape[1]
    bl = MOBA_BLOCK
    n_pairs = (qi + 1) // 2
    own_slot = s_ref.shape[1] - 1

    qt = qt_ref[0]
    feat = lax.broadcasted_iota(jnp.int32, qt.shape, 0)
    head_lo = feat < DH_A
    qth = (jnp.where(head_lo, qt, jnp.zeros_like(qt)), jnp.where(head_lo, jnp.zeros_like(qt), qt))
    km = kmean_ref[0].astype(bf16)
    blk = lax.broadcasted_iota(jnp.int32, (nb, bl), 0)
    n_valid = jnp.minimum(qi, MOBA_TOPK)

    key = lax.broadcasted_iota(jnp.int32, (bl, bl), 0)
    qry = lax.broadcasted_iota(jnp.int32, (bl, bl), 1)
    own0 = pl.multiple_of(qi * bl, bl)
    k_own = k_ref[pl.ds(own0, bl), :]
    vt_pair = vt_ref[qi // 2]
    vt_own = jnp.where(qi % 2 == 0, vt_pair[:, 0:bl], vt_pair[:, bl:2 * bl])
    m_own = []
    for hh in range(2):
        s = jnp.where(blk < qi, _dot(km, qth[hh]), NEG_INF)
        chosen = jnp.zeros(s.shape, f32)
        for r in range(MOBA_TOPK):
            mx = jnp.max(s, axis=0, keepdims=True)
            first = jnp.min(jnp.where(s == mx, blk, nb), axis=0, keepdims=True)
            pick = blk == first
            chosen = jnp.where(pick, jnp.where(r < n_valid, 1.0, 0.0), chosen)
            s = jnp.where(pick, -jnp.inf, s)
        sel_ref[hh] = chosen
        so = jnp.where(key <= qry, _dot(k_own, qth[hh]), NEG_INF)
        s_ref[hh, own_slot, 0:bl, :] = so
        m_own.append(jnp.max(so, axis=0, keepdims=True))

    def scores(j2, ms):
        j0 = pl.multiple_of(j2 * 2 * bl, 2 * bl)
        kj = k_ref[pl.ds(j0, 2 * bl), :]
        out = []
        for hh in range(2):
            s = _dot(kj, qth[hh])
            m = ms[hh]
            for t in range(2):
                picked = sel_ref[hh, pl.ds(2 * j2 + t, 1), :] > 0.0
                st = jnp.where(picked, s[t * bl:(t + 1) * bl], NEG_INF)
                s_ref[hh, j2, t * bl:(t + 1) * bl, :] = st
                m = jnp.maximum(m, jnp.max(st, axis=0, keepdims=True))
            out.append(m)
        return tuple(out)

    ms = lax.fori_loop(0, n_pairs, scores, tuple(m_own))

    l0, acc0 = [], []
    for hh in range(2):
        p = jnp.exp2(s_ref[hh, own_slot, 0:bl, :] - ms[hh])
        l0.append(jnp.sum(p, axis=0, keepdims=True))
        acc0.append(_dot(vt_own, p.astype(bf16)))

    def values(j2, carry):
        vtj = vt_ref[j2]
        ls, accs = carry
        new_l, new_acc = [], []
        for hh in range(2):
            p = jnp.exp2(s_ref[hh, j2] - ms[hh])
            new_l.append(ls[hh] + jnp.sum(p, axis=0, keepdims=True))
            new_acc.append(accs[hh] + _dot(vtj, p.astype(bf16)))
        return tuple(new_l), tuple(new_acc)

    ls, accs = lax.fori_loop(0, n_pairs, values, (tuple(l0), tuple(acc0)))
    out_t = jnp.where(head_lo, accs[0] / ls[0], accs[1] / ls[1])
    o_ref[...] = jnp.transpose(out_t).astype(o_ref.dtype)


def _attn_prompt(qt, k, vt, kmean, bsz, t_len):
    nb = t_len // MOBA_BLOCK
    bl = MOBA_BLOCK
    pair = 2 * DH_A
    tile = qt.shape[2]
    assert tile == 2 * bl and t_len % tile == 0
    n_tiles = t_len // tile
    slots = n_tiles + 1
    return pl.pallas_call(
        _attn_prompt_kernel,
        grid=(bsz, W_A // pair, nb),
        in_specs=[pl.BlockSpec((1, pair, bl), lambda b, hp, qi: (b * n_tiles + qi // 2, hp, qi % 2)),
                  pl.BlockSpec((t_len, pair), lambda b, hp, qi: (b, hp)),
                  pl.BlockSpec((n_tiles, pair, tile), lambda b, hp, qi: (b, hp, 0)),
                  pl.BlockSpec((1, nb, pair), lambda b, hp, qi: (b, 0, hp))],
        out_specs=pl.BlockSpec((bl, pair), lambda b, hp, qi: (b * nb + qi, hp)),
        out_shape=jax.ShapeDtypeStruct((bsz * t_len, W_A), bf16),
        scratch_shapes=[
            pltpu.VMEM((2, nb, bl), f32),
            pltpu.VMEM((2, slots, tile, bl), f32),
        ],
        compiler_params=_cparams(("arbitrary", "arbitrary", "arbitrary"), 48),
        name="attn_prompt",
    )(qt, k, vt, kmean)


def _attn_select_kernel(layer, pt_ref, q_ref, cache_ref, idx_ref, kbuf_ref, sem_ref):
    b = pl.program_id(0)
    n_pages, page = kbuf_ref.shape[0], kbuf_ref.shape[3]
    per_block = MOBA_BLOCK // page
    nbk = n_pages // per_block

    def page_copy(p):
        return pltpu.make_async_copy(cache_ref.at[layer, pt_ref[b, p]], kbuf_ref.at[p], sem_ref.at[0])

    for p in range(n_pages):
        page_copy(p).start()
    for p in range(n_pages):
        page_copy(p).wait()

    qb = jnp.broadcast_to(q_ref[0], (H_A, DH_A, page))
    scores = []
    for j in range(nbk):
        tok = jnp.sum(kbuf_ref[j * per_block] * qb, axis=1)
        for t in range(1, per_block):
            tok = tok + jnp.sum(kbuf_ref[j * per_block + t] * qb, axis=1)
        scores.append(jnp.sum(tok, axis=1, keepdims=True) * (1.0 / MOBA_BLOCK))

    idx_ref[0] = jnp.zeros(idx_ref.shape[1:], jnp.int32)
    for r in range(MOBA_TOPK):
        mx = functools.reduce(jnp.maximum, scores)
        first = jnp.full(mx.shape, nbk, jnp.int32)
        for j in reversed(range(nbk)):
            first = jnp.where(scores[j] == mx, j, first)
        idx_ref[0, :, r:r + 1] = first
        scores = [jnp.where(first == j, -jnp.inf, s) for j, s in enumerate(scores)]


def _attn_select(page_table, q_col, cache_kt, layer):
    bsz, n_pages = page_table.shape
    page = cache_kt.shape[4]
    grid_spec = pltpu.PrefetchScalarGridSpec(
        num_scalar_prefetch=1,
        grid=(bsz,),
        in_specs=[
            pl.BlockSpec((1, H_A, DH_A, 1), lambda b, pt: (b, 0, 0, 0)),
            pl.BlockSpec(memory_space=pl.ANY),
        ],
        out_specs=pl.BlockSpec((1, H_A, LANES), lambda b, pt: (b, 0, 0)),
        scratch_shapes=[pltpu.VMEM((n_pages, H_A, DH_A, page), f32), pltpu.SemaphoreType.DMA((1,))],
    )
    return pl.pallas_call(
        functools.partial(_attn_select_kernel, layer),
        grid_spec=grid_spec,
        out_shape=jax.ShapeDtypeStruct((bsz, H_A, LANES), jnp.int32),
        compiler_params=_cparams(("arbitrary",), 40),
        name="attn_select",
    )(page_table, q_col, cache_kt)


def _attn_sample_kernel(layer, pt_ref, sel_ref, q_ref, kn_ref, vn_ref, ck_ref, cv_ref, o_ref,
                        kbuf_ref, vbuf_ref, sem_ref):
    b = pl.program_id(0)
    page = ck_ref.shape[4]
    per_block = MOBA_BLOCK // page

    def copies(h, r, t):
        pg = pt_ref[b, sel_ref[b, h * MOBA_TOPK + r] * per_block + t]
        dst = pl.ds((r * per_block + t) * page, page)
        return (pltpu.make_async_copy(ck_ref.at[layer, pg, h], kbuf_ref.at[h, :, dst], sem_ref.at[0]),
                pltpu.make_async_copy(cv_ref.at[layer, pg, h], vbuf_ref.at[h, :, dst], sem_ref.at[1]))

    slots = [(h, r, t) for h in range(H_A) for r in range(MOBA_TOPK) for t in range(per_block)]
    for s in slots:
        ck, cv = copies(*s)
        ck.start()
        cv.start()
    for s in slots:
        ck, cv = copies(*s)
        ck.wait()
        cv.wait()

    for h in range(H_A):
        qh = q_ref[0, h:h + 1, :]
        kh = kbuf_ref[h].astype(bf16)
        vh = vbuf_ref[h].astype(bf16)
        q8 = jnp.broadcast_to(qh, (SUBLANES, DH_A)).astype(bf16)
        s = _dot(q8, kh)
        s_own = jnp.sum(qh * kn_ref[0, h:h + 1, :], axis=1, keepdims=True)
        m = jnp.maximum(jnp.max(s, axis=1, keepdims=True), s_own)
        p = jnp.exp2(s - m)
        p_own = jnp.exp2(s_own - m)
        den = jnp.sum(p, axis=1, keepdims=True) + p_own
        out = (_dot_nt(p.astype(bf16), vh) + p_own * vn_ref[0, h:h + 1, :]) / den
        o_ref[0, h:h + 1, :] = out[0:1, :].astype(o_ref.dtype)


def _attn_sample(page_table, sel, q, k_new, v_new, cache_k, cache_v, layer):
    bsz = page_table.shape[0]
    n_sel = MOBA_TOPK * MOBA_BLOCK
    tok = pl.BlockSpec((1, H_A, DH_A), lambda b, pt, s: (b, 0, 0))
    grid_spec = pltpu.PrefetchScalarGridSpec(
        num_scalar_prefetch=2,
        grid=(bsz,),
        in_specs=[tok, tok, tok, pl.BlockSpec(memory_space=pl.ANY), pl.BlockSpec(memory_space=pl.ANY)],
        out_specs=tok,
        scratch_shapes=[pltpu.VMEM((H_A, DH_A, n_sel), f32), pltpu.VMEM((H_A, DH_A, n_sel), f32),
                        pltpu.SemaphoreType.DMA((2,))],
    )
    return pl.pallas_call(
        functools.partial(_attn_sample_kernel, layer),
        grid_spec=grid_spec,
        out_shape=jax.ShapeDtypeStruct((bsz, H_A, DH_A), bf16),
        compiler_params=_cparams(("arbitrary",), 32),
        name="attn_sample",
    )(page_table, sel, q, k_new, v_new, cache_k, cache_v)


def _gla_tables():
    c = GLA_CHUNK
    tri = np.tril(np.ones((c, c), np.float32))
    mats = [tri]
    masks = []
    t = np.arange(c)
    for lev in range(GLA_LEVELS):
        h = 1 << lev
        mid = (t // (2 * h)) * (2 * h) + h
        mats.append(tri[mid - 1])
        same = (t[:, None] // (2 * h)) == (t[None, :] // (2 * h))
        masks.append(same & ((t[:, None] & h) != 0) & ((t[None, :] & h) == 0))
    masks.append(np.eye(c, dtype=bool))
    sel = np.concatenate(mats, axis=0)
    lmask = np.stack([np.tile(mk, (H_C, 1)) for mk in masks]).astype(np.float32)
    return sel, lmask


def _gla_prompt_kernel(qk_ref, la_ref, v_ref, r_ref, g_ref, s0_ref, sel_ref, lmask_ref,
                       y_ref, sout_ref, s_ref):
    c = GLA_CHUNK

    @pl.when(pl.program_id(1) == 0)
    def _():
        s_ref[...] = s0_ref[0]

    sel = sel_ref[...]
    g = g_ref[...]
    lane = lax.broadcasted_iota(jnp.int32, (c, QK_C), 1)
    rowc = lax.broadcasted_iota(jnp.int32, (c, QK_C), 0)
    head_of_lane = lane // DK_C
    head_of_row = lax.broadcasted_iota(jnp.int32, (QK_C, DV_C), 0) // DK_C

    def stack_heads(a):
        return jnp.concatenate([jnp.where(head_of_lane == h, a, 0.0) for h in range(H_C)], axis=0)

    def chunk(ci, carry):
        r0 = pl.multiple_of(ci * c, c)
        la = la_ref[pl.ds(r0, c), :]
        la1 = la.astype(bf16)
        rem = la - la1.astype(f32)
        la2 = rem.astype(bf16)
        la3 = (rem - la2.astype(f32)).astype(bf16)
        ball = _dot(sel, la1) + _dot(sel, la2) + _dot(sel, la3)
        b = ball[0:c]
        q = qk_ref[pl.ds(r0, c), 0:QK_C]
        k = qk_ref[pl.ds(r0, c), QK_C:2 * QK_C]
        v = v_ref[pl.ds(r0, c), :].astype(bf16)

        att = lmask_ref[GLA_LEVELS] * _dot_nt(stack_heads(q).astype(bf16), k.astype(bf16))
        for lev in range(GLA_LEVELS):
            ref_b = ball[(lev + 1) * c:(lev + 2) * c]
            right = (rowc & (1 << lev)) != 0
            qt = q * jnp.exp(jnp.where(right, b - ref_b, NEG_INF))
            kt = k * jnp.exp(jnp.where(right, NEG_INF, ref_b - b))
            att = att + lmask_ref[lev] * _dot_nt(stack_heads(qt).astype(bf16), kt.astype(bf16))

        s_old = s_ref[...]
        inter = _dot(stack_heads(q * jnp.exp(b)).astype(bf16), s_old.astype(bf16))
        b_last = b[c - 1:c, :]
        kt_t = jnp.transpose(k * jnp.exp(b_last - b))
        b_t = jnp.transpose(b)
        upd = _dot(kt_t.astype(bf16), v)
        s_new = jnp.exp(b_t[:, c - 1:c]) * s_old
        for h in range(H_C):
            s_new = s_new + jnp.where(head_of_row == h, upd[:, h * DV_C:(h + 1) * DV_C], 0.0)
        s_ref[...] = s_new

        att_bf = att.astype(bf16)
        for h in range(H_C):
            o = (_dot(att_bf[h * c:(h + 1) * c], v[:, h * DV_C:(h + 1) * DV_C])
                 + inter[h * c:(h + 1) * c])
            mu = jnp.mean(o, axis=-1, keepdims=True)
            oc = o - mu
            var = jnp.mean(oc * oc, axis=-1, keepdims=True)
            hs = slice(h * DV_C, (h + 1) * DV_C)
            gate = g[:, hs] * _silu(r_ref[pl.ds(r0, c), hs])
            y_ref[pl.ds(r0, c), hs] = (oc * lax.rsqrt(var + LN_EPS) * gate).astype(y_ref.dtype)
        return carry

    lax.fori_loop(0, qk_ref.shape[0] // c, chunk, 0)
    sout_ref[0] = s_ref[...]


def _gla_prompt(qk, la, v, r, g_gla, s0, bsz, t_len):
    tb = min(GLA_TB, t_len)
    nt = t_len // tb
    sel_np, lmask_np = _gla_tables()
    sel = jnp.asarray(sel_np, bf16)
    lmask = jnp.asarray(lmask_np, f32)
    row = lambda n: pl.BlockSpec((tb, n), lambda b, t: (b * nt + t, 0))
    st = pl.BlockSpec((1, QK_C, DV_C), lambda b, t: (b, 0, 0))
    return pl.pallas_call(
        _gla_prompt_kernel,
        grid=(bsz, nt),
        in_specs=[row(2 * QK_C), row(QK_C), row(V_C), row(V_C),
                  pl.BlockSpec((1, V_C), lambda b, t: (0, 0)), st,
                  pl.BlockSpec(sel.shape, lambda b, t: (0, 0)),
                  pl.BlockSpec(lmask.shape, lambda b, t: (0, 0, 0))],
        out_specs=[row(V_C), st],
        out_shape=[jax.ShapeDtypeStruct((bsz * t_len, V_C), bf16),
                   jax.ShapeDtypeStruct((bsz, QK_C, DV_C), f32)],
        scratch_shapes=[pltpu.VMEM((QK_C, DV_C), f32)],
        compiler_params=_cparams(("arbitrary", "arbitrary"), 40),
        name="gla_prompt",
    )(qk, la, v, r, g_gla, s0, sel, lmask)


def _gla_step_kernel(q_ref, k_ref, la_ref, v_ref, r_ref, g_ref, s_ref, y_ref, sout_ref):
    head_of_row = lax.broadcasted_iota(jnp.int32, (QK_C, DV_C), 0) // DK_C
    v = v_ref[0]
    v_rows = jnp.zeros((QK_C, DV_C), f32)
    for h in range(H_C):
        v_rows = jnp.where(head_of_row == h, v[:, h * DV_C:(h + 1) * DV_C], v_rows)
    s_new = jnp.exp(la_ref[0]) * s_ref[0] + k_ref[0] * v_rows
    sout_ref[0] = s_new
    qs = q_ref[0] * s_new
    g = g_ref[...]
    for h in range(H_C):
        o = jnp.sum(qs[h * DK_C:(h + 1) * DK_C], axis=0, keepdims=True)
        mu = jnp.mean(o, axis=-1, keepdims=True)
        oc = o - mu
        var = jnp.mean(oc * oc, axis=-1, keepdims=True)
        hs = slice(h * DV_C, (h + 1) * DV_C)
        y_ref[0, :, hs] = (oc * lax.rsqrt(var + LN_EPS) * g[:, hs] * _silu(r_ref[0, :, hs])).astype(y_ref.dtype)


def _gla_step(q_col, k_col, la_col, v, r, g_gla, s):
    bsz = s.shape[0]
    col = pl.BlockSpec((1, QK_C, 1), lambda b: (b, 0, 0))
    tok = pl.BlockSpec((1, 1, V_C), lambda b: (b, 0, 0))
    st = pl.BlockSpec((1, QK_C, DV_C), lambda b: (b, 0, 0))
    return pl.pallas_call(
        _gla_step_kernel,
        grid=(bsz,),
        in_specs=[col, col, col, tok, tok, pl.BlockSpec((1, V_C), lambda b: (0, 0)), st],
        out_specs=[tok, st],
        out_shape=[jax.ShapeDtypeStruct((bsz, 1, V_C), bf16), jax.ShapeDtypeStruct(s.shape, f32)],
        compiler_params=_cparams(("arbitrary",), 16),
        name="gla_step",
    )(q_col, k_col, la_col, v, r, g_gla, s)


def _merge_kernel(xb_ref, xf_ref, ya_ref, yb_ref, yc_ref, wg_ref, wa_ref, wb_ref, wc_ref, wo_ref,
                  g_ref, b_ref, hf_ref, hb_ref):
    xb = xb_ref[...]
    merged = None
    for i, (y_ref, w_ref) in enumerate(((ya_ref, wa_ref), (yb_ref, wb_ref), (yc_ref, wc_ref))):
        gate = jax.nn.sigmoid(_dot(xb, wg_ref[:, i * D_MODEL:(i + 1) * D_MODEL]))
        term = gate * _dot(y_ref[...], w_ref[...])
        merged = term if merged is None else merged + term
    mix = _dot(merged.astype(bf16), wo_ref[...])
    h = _layer_norm_rows(ALPHA * xf_ref[...] + mix, g_ref[...], b_ref[...])
    hf_ref[...] = h
    hb_ref[...] = h.astype(bf16)


def _merge(x_bf, x_f, y_a, y_b, y_c, w_gate, w_a, w_b, w_c, w_o, ln_g, ln_b, tm):
    m = x_bf.shape[0]
    row = lambda n: pl.BlockSpec((tm, n), lambda i: (i, 0))
    full = lambda a: pl.BlockSpec(a.shape, lambda i: (0,) * a.ndim)
    return pl.pallas_call(
        _merge_kernel,
        grid=(m // tm,),
        in_specs=[row(D_MODEL), row(D_MODEL), row(W_A), row(W_B), row(V_C),
                  full(w_gate), full(w_a), full(w_b), full(w_c), full(w_o), full(ln_g), full(ln_b)],
        out_specs=[row(D_MODEL), row(D_MODEL)],
        out_shape=[jax.ShapeDtypeStruct((m, D_MODEL), f32), jax.ShapeDtypeStruct((m, D_MODEL), bf16)],
        compiler_params=_cparams(("arbitrary",), 56),
        name="merge",
    )(x_bf, x_f, y_a, y_b, y_c, w_gate, w_a, w_b, w_c, w_o, ln_g, ln_b)


def _group_probs(logits):
    lane = lax.broadcasted_iota(jnp.int32, logits.shape, 1)
    gl = jnp.where(lane < N_GROUPS, logits, -jnp.inf)
    pg = jnp.exp(gl - jnp.max(gl, axis=1, keepdims=True))
    pg = pg / jnp.sum(pg, axis=1, keepdims=True)
    p_top = jnp.max(pg, axis=1, keepdims=True)
    g_idx = jnp.min(jnp.where(pg == p_top, lane, LANES), axis=1, keepdims=True)
    return pg, g_idx


def _combine_weights(logits, pg, g_idx):
    lane = lax.broadcasted_iota(jnp.int32, logits.shape, 1)
    p_top = jnp.sum(jnp.where(lane == g_idx, pg, 0.0), axis=1, keepdims=True)
    e0 = N_GROUPS + g_idx * EXPERTS_PER_GROUP
    el = jnp.where((lane >= e0) & (lane < e0 + EXPERTS_PER_GROUP), logits, -jnp.inf)
    pe = jnp.exp(el - jnp.max(el, axis=1, keepdims=True))
    pe = pe / jnp.sum(pe, axis=1, keepdims=True)
    p1 = jnp.max(pe, axis=1, keepdims=True)
    i1 = jnp.min(jnp.where(pe == p1, lane, LANES), axis=1, keepdims=True)
    rest = jnp.where(lane == i1, -1.0, pe)
    p2 = jnp.max(rest, axis=1, keepdims=True)
    i2 = jnp.min(jnp.where(rest == p2, lane, LANES), axis=1, keepdims=True)
    den = p1 + p2
    return jnp.where(lane == i1, p_top * p1 / den, jnp.where(lane == i2, p_top * p2 / den, 0.0))


def _moe_kernel(hb_ref, hf_ref, wr_ref, br_ref, wg_ref, wu_ref, wd_ref, g_ref, b_ref,
                yf_ref, yb_ref, comb_ref, acc_ref):
    e = pl.program_id(1)
    hb = hb_ref[...]

    @pl.when(e == 0)
    def _():
        logits = _dot(hb, wr_ref[...]) + br_ref[...]
        pg, g_idx = _group_probs(logits)
        comb_ref[...] = _combine_weights(logits, pg, g_idx)
        acc_ref[...] = jnp.zeros_like(acc_ref)

    comb = comb_ref[...]
    lane = lax.broadcasted_iota(jnp.int32, comb.shape, 1)
    cw = jnp.sum(jnp.where(lane == N_GROUPS + e, comb, 0.0), axis=1, keepdims=True)
    act = _silu(_dot(hb, wg_ref[0])) * _dot(hb, wu_ref[0]) * cw
    acc_ref[...] += _dot(act.astype(bf16), wd_ref[0])

    @pl.when(e == N_EXPERTS - 1)
    def _():
        y = _layer_norm_rows(ALPHA * hf_ref[...] + acc_ref[...], g_ref[...], b_ref[...])
        yf_ref[...] = y
        yb_ref[...] = y.astype(bf16)


def _moe(h_bf, h_f, w_router, b_router, w_gate, w_up, w_down, ln_g, ln_b, tm):
    m = h_bf.shape[0]
    row = lambda n: pl.BlockSpec((tm, n), lambda i, e: (i, 0))
    full = lambda a: pl.BlockSpec(a.shape, lambda i, e: (0,) * a.ndim)
    return pl.pallas_call(
        _moe_kernel,
        grid=(m // tm, N_EXPERTS),
        in_specs=[row(D_MODEL), row(D_MODEL), full(w_router), full(b_router),
                  pl.BlockSpec((1, D_MODEL, D_EXPERT), lambda i, e: (e, 0, 0)),
                  pl.BlockSpec((1, D_MODEL, D_EXPERT), lambda i, e: (e, 0, 0)),
                  pl.BlockSpec((1, D_EXPERT, D_MODEL), lambda i, e: (e, 0, 0)),
                  full(ln_g), full(ln_b)],
        out_specs=[row(D_MODEL), row(D_MODEL)],
        out_shape=[jax.ShapeDtypeStruct((m, D_MODEL), f32), jax.ShapeDtypeStruct((m, D_MODEL), bf16)],
        scratch_shapes=[pltpu.VMEM((tm, LANES), f32), pltpu.VMEM((tm, D_MODEL), f32)],
        compiler_params=_cparams(("arbitrary", "arbitrary"), 48),
        name="moe",
    )(h_bf, h_f, w_router, b_router, w_gate, w_up, w_down, ln_g, ln_b)


MOE_TILE = 512
PERMUTE_CHUNK = 64


def _route_kernel(hb_ref, wr_ref, br_ref, tri_ref, g_ref, rank_ref, cnt_ref, run_ref):
    @pl.when(pl.program_id(0) == 0)
    def _():
        run_ref[...] = jnp.zeros_like(run_ref)

    logits = _dot(hb_ref[...], wr_ref[...]) + br_ref[...]
    _, g_idx = _group_probs(logits)
    lane = lax.broadcasted_iota(jnp.int32, logits.shape, 1)
    onehot = jnp.where(lane == g_idx, 1.0, 0.0)
    earlier = _dot(tri_ref[...], onehot.astype(bf16)) + run_ref[...]
    g_ref[...] = g_idx
    rank_ref[...] = jnp.sum(onehot * earlier, axis=1, keepdims=True).astype(jnp.int32)
    run_ref[...] = run_ref[...] + jnp.sum(onehot, axis=0, keepdims=True)
    cnt_ref[...] = run_ref[...]


def _route(h_bf, w_router, b_router, tm):
    m = h_bf.shape[0]
    tri = jnp.asarray(np.tril(np.ones((tm, tm), np.float32), -1), bf16)
    col = pl.BlockSpec((tm, 1), lambda i: (i, 0))
    full = lambda a: pl.BlockSpec(a.shape, lambda i: (0,) * a.ndim)
    return pl.pallas_call(
        _route_kernel,
        grid=(m // tm,),
        in_specs=[pl.BlockSpec((tm, D_MODEL), lambda i: (i, 0)), full(w_router), full(b_router), full(tri)],
        out_specs=[col, col, pl.BlockSpec((1, LANES), lambda i: (0, 0))],
        out_shape=[jax.ShapeDtypeStruct((m, 1), jnp.int32), jax.ShapeDtypeStruct((m, 1), jnp.int32),
                   jax.ShapeDtypeStruct((1, LANES), f32)],
        scratch_shapes=[pltpu.VMEM((1, LANES), f32)],
        compiler_params=_cparams(("arbitrary",), 16),
        name="moe_route",
    )(h_bf, w_router, b_router, tri)


def _permute_kernel(index_dst, n, idx_ref, src_ref, *rest):
    dst_ref, sem_ref = rest[-2], rest[-1]
    ch = PERMUTE_CHUNK

    def row_copy(t):
        s, d = (t, idx_ref[t]) if index_dst else (idx_ref[t], t)
        return pltpu.make_async_copy(src_ref.at[pl.ds(s, 1), :], dst_ref.at[pl.ds(d, 1), :], sem_ref.at[0])

    def chunk(c, carry):
        base = c * ch
        for u in range(ch):
            row_copy(base + u).start()

        @pl.when(c > 0)
        def _():
            for u in range(ch):
                row_copy(base - ch + u).wait()
        return carry

    lax.fori_loop(0, n // ch, chunk, 0)
    for u in range(ch):
        row_copy(n - ch + u).wait()


def _permute_rows(idx, src, n, index_dst, init=None):
    assert n % PERMUTE_CHUNK == 0
    any_spec = pl.BlockSpec(memory_space=pl.ANY)
    args = [idx, src] + ([init] if init is not None else [])
    out_rows = init.shape[0] if init is not None else n
    grid_spec = pltpu.PrefetchScalarGridSpec(
        num_scalar_prefetch=1,
        grid=(1,),
        in_specs=[any_spec] * (len(args) - 1),
        out_specs=any_spec,
        scratch_shapes=[pltpu.SemaphoreType.DMA((1,))],
    )
    return pl.pallas_call(
        functools.partial(_permute_kernel, index_dst, n),
        grid_spec=grid_spec,
        out_shape=jax.ShapeDtypeStruct((out_rows, src.shape[1]), src.dtype),
        input_output_aliases={2: 0} if init is not None else {},
        compiler_params=_cparams(("arbitrary",), 16),
        name="moe_permute",
    )(*args)


def _gmoe_kernel(tg_ref, na_ref, hs_ref, wr_ref, br_ref, wg_ref, wu_ref, wd_ref, g_ref, b_ref, ys_ref):
    i = pl.program_id(0)
    grp = tg_ref[i]

    @pl.when(i < na_ref[0])
    def _():
        hf = hs_ref[...]
        hb = hf.astype(bf16)
        logits = _dot(hb, wr_ref[...]) + br_ref[...]
        pg, _ = _group_probs(logits)
        comb = _combine_weights(logits, pg, grp)
        lane = lax.broadcasted_iota(jnp.int32, comb.shape, 1)
        acc = None
        for e in range(EXPERTS_PER_GROUP):
            cw = jnp.sum(jnp.where(lane == N_GROUPS + grp * EXPERTS_PER_GROUP + e, comb, 0.0),
                         axis=1, keepdims=True)
            act = _silu(_dot(hb, wg_ref[0, e])) * _dot(hb, wu_ref[0, e]) * cw
            term = _dot(act.astype(bf16), wd_ref[0, e])
            acc = term if acc is None else acc + term
        ys_ref[...] = _layer_norm_rows(ALPHA * hf + acc, g_ref[...], b_ref[...])

    @pl.when(i >= na_ref[0])
    def _():
        ys_ref[...] = jnp.zeros_like(ys_ref)


def _gmoe(tile_group, n_active, hs, w_router, b_router, w_gate, w_up, w_down, ln_g, ln_b):
    n_tiles = hs.shape[0] // MOE_TILE
    grouped = lambda w: w.reshape(N_GROUPS, EXPERTS_PER_GROUP, *w.shape[1:])
    wspec = lambda k, n: pl.BlockSpec((1, EXPERTS_PER_GROUP, k, n), lambda i, tg, na: (tg[i], 0, 0, 0))
    full = lambda a: pl.BlockSpec(a.shape, lambda i, tg, na: (0,) * a.ndim)
    row = pl.BlockSpec((MOE_TILE, D_MODEL), lambda i, tg, na: (i, 0))
    grid_spec = pltpu.PrefetchScalarGridSpec(
        num_scalar_prefetch=2,
        grid=(n_tiles,),
        in_specs=[row, full(w_router), full(b_router), wspec(D_MODEL, D_EXPERT), wspec(D_MODEL, D_EXPERT),
                  wspec(D_EXPERT, D_MODEL), full(ln_g), full(ln_b)],
        out_specs=row,
    )
    return pl.pallas_call(
        _gmoe_kernel,
        grid_spec=grid_spec,
        out_shape=jax.ShapeDtypeStruct(hs.shape, f32),
        compiler_params=_cparams(("arbitrary",), 56),
        name="moe_grouped",
    )(tile_group, n_active, hs, w_router, b_router, grouped(w_gate), grouped(w_up), grouped(w_down), ln_g, ln_b)


def _moe_routed(h_bf, h_f, w_router, b_router, w_gate, w_up, w_down, ln_g, ln_b, tm):
    m = h_bf.shape[0]
    g, rank, cnt = _route(h_bf, w_router, b_router, tm)
    counts = cnt[0, :N_GROUPS].astype(jnp.int32)
    tiles_per_group = (counts + MOE_TILE - 1) // MOE_TILE
    tile_end = jnp.cumsum(tiles_per_group)
    start = (tile_end - tiles_per_group) * MOE_TILE
    dest = start[g[:, 0]] + rank[:, 0]
    n_tiles = m // MOE_TILE + N_GROUPS
    tile_group = jnp.minimum(
        jnp.sum(jnp.arange(n_tiles, dtype=jnp.int32)[:, None] >= tile_end[None, :], axis=1), N_GROUPS - 1
    ).astype(jnp.int32)
    hs = _permute_rows(dest, h_f, m, True, init=jnp.zeros((n_tiles * MOE_TILE, D_MODEL), f32))
    ys = _gmoe(tile_group, tile_end[N_GROUPS - 1:], hs, w_router, b_router, w_gate, w_up, w_down, ln_g, ln_b)
    y_f = _permute_rows(dest, ys, m, False)
    return y_f, y_f.astype(bf16)


def _row_tile(m):
    for tm in (512, 256, 128, 64, 32, 16, 8):
        if m % tm == 0:
            return tm
    raise ValueError(f"row count {m} must be a multiple of {SUBLANES}")


def _layer_weights(l, w_in, conv_w, w_fg2, b_fg, g_gla, w_br_a, w_br_b, w_br_c, w_out,
                   ln1_g, ln1_b, ln2_g, ln2_b, w_rg, b_rg, w_re, b_re, w_e_gate, w_e_up, w_e_down):
    wi = w_in[l]
    n_front = _P_FL
    gate0 = n_front + FG_RANK
    w_proj = jnp.concatenate(
        [wi[:, :n_front], jnp.pad(wi[:, n_front:gate0], ((0, 0), (0, LANES - FG_RANK)))], axis=1).astype(bf16)
    pad_r = LANES - N_GROUPS - N_EXPERTS
    return dict(
        w_proj=w_proj,
        w_gate=wi[:, gate0:].astype(bf16),
        wfg=jnp.pad(w_fg2[l], ((0, LANES - FG_RANK), (0, 0))).astype(bf16),
        bfg=b_fg[l][None, :],
        conv_w=conv_w[l],
        g_gla=g_gla[l][None, :],
        w_a=w_br_a[l].astype(bf16), w_b=w_br_b[l].astype(bf16), w_c=w_br_c[l].astype(bf16),
        w_o=w_out[l].astype(bf16),
        ln1_g=ln1_g[l][None, :], ln1_b=ln1_b[l][None, :], ln2_g=ln2_g[l][None, :], ln2_b=ln2_b[l][None, :],
        w_router=jnp.pad(jnp.concatenate([w_rg[l], w_re[l]], axis=1), ((0, 0), (0, pad_r))).astype(bf16),
        b_router=jnp.pad(jnp.concatenate([b_rg[l], b_re[l]]), (0, pad_r))[None, :],
        w_e_gate=w_e_gate[l].astype(bf16), w_e_up=w_e_up[l].astype(bf16), w_e_down=w_e_down[l].astype(bf16),
    )


def _finish_layer(p, x_bf, x_f, y_a, y_b, y_c, tm):
    h_f, h_bf = _merge(x_bf, x_f, y_a, y_b, y_c, p["w_gate"], p["w_a"], p["w_b"], p["w_c"], p["w_o"],
                       p["ln1_g"], p["ln1_b"], tm)
    moe = _moe_routed if h_f.shape[0] >= N_GROUPS * MOE_TILE else _moe
    return moe(h_bf, h_f, p["w_router"], p["b_router"], p["w_e_gate"], p["w_e_up"], p["w_e_down"],
               p["ln2_g"], p["ln2_b"], tm)


def _prompt_layer(p, x_f, x_bf, bsz, t_len):
    tm = _row_tile(t_len)
    q, k, v, y_b, u_tail, qk, vc, rc, la, k_bf, v_bf, kmean = _proj(
        x_bf, p["w_proj"], p["wfg"], p["bfg"], p["conv_w"], tm, t_len // tm)
    y_a = _attn_prompt(q, k_bf, v_bf, kmean.reshape(bsz, t_len // MOBA_BLOCK, W_A), bsz, t_len)
    y_c, s_fin = _gla_prompt(qk, la, vc, rc, p["g_gla"], jnp.zeros((bsz, QK_C, DV_C), f32), bsz, t_len)
    x_f, x_bf = _finish_layer(p, x_bf, x_f, y_a, y_b, y_c, tm)
    conv_state = u_tail.reshape(bsz, t_len // tm, SUBLANES, W_B)[:, -1, SUBLANES - (CONV_W - 1):, :]
    return (x_f, x_bf, k.reshape(bsz, t_len, H_A, DH_A), v.reshape(bsz, t_len, H_A, DH_A), conv_state,
            s_fin.reshape(bsz, H_C, DK_C, DV_C))


def _sample_layer(p, layer, x_f, x_bf, cache_kt, cache_vt, conv_prev, gla_prev, page_table):
    bsz = x_f.shape[0]
    q, k, v, y_b, u, qk, vc, rc, la = _proj(
        x_bf, p["w_proj"], p["wfg"], p["bfg"], p["conv_w"], bsz, 0, prev=(conv_prev[:, 0], conv_prev[:, 1]))
    heads = lambda a: a.reshape(bsz, H_A, DH_A)
    q_h = heads(q.astype(f32))
    idx = _attn_select(page_table, q_h[..., None], cache_kt, layer)
    sel = idx[:, :, :MOBA_TOPK].reshape(bsz, H_A * MOBA_TOPK)
    y_a = _attn_sample(page_table, sel, q_h, heads(k), heads(v), cache_kt, cache_vt, layer)
    y_c, s_new = _gla_step(qk[:, :QK_C, None], qk[:, QK_C:, None], la[:, :, None], vc[:, None, :],
                           rc[:, None, :], p["g_gla"], gla_prev.reshape(bsz, QK_C, DV_C))
    x_f, x_bf = _finish_layer(p, x_bf, x_f, y_a.reshape(bsz, W_A), y_b, y_c.reshape(bsz, V_C), bsz)
    conv_state = jnp.stack([conv_prev[:, 1], u], axis=1)
    return (x_f, x_bf, k.reshape(bsz, 1, H_A, DH_A), v.reshape(bsz, 1, H_A, DH_A), conv_state,
            s_new.reshape(bsz, H_C, DK_C, DV_C))


def kernel(x_prompt, x_sample, cache_k, cache_v, state_conv, state_gla, page_table, w_in, conv_w, w_fg2, b_fg, g_gla, w_br_a, w_br_b, w_br_c, w_out, ln1_g, ln1_b, ln2_g, ln2_b, w_rg, b_rg, w_re, b_re, w_e_gate, w_e_up, w_e_down):
    bsz, t_len, _ = x_prompt.shape
    dec_b = x_sample.shape[0]
    assert x_sample.shape[1] == 1 and t_len % MOBA_BLOCK == 0 and dec_b % SUBLANES == 0
    assert (page_table.shape[1] * cache_k.shape[2]) % MOBA_BLOCK == 0 and MOBA_BLOCK % cache_k.shape[2] == 0
    cache_kt = jnp.transpose(cache_k, (0, 1, 3, 4, 2))
    cache_vt = jnp.transpose(cache_v, (0, 1, 3, 4, 2))
    xp_f = x_prompt.reshape(bsz * t_len, D_MODEL)
    xs_f = x_sample.reshape(dec_b, D_MODEL)
    xp_bf, xs_bf = xp_f.astype(bf16), xs_f.astype(bf16)
    outs_p, outs_s = [], []
    for l in range(DEPTH):
        p = _layer_weights(l, w_in, conv_w, w_fg2, b_fg, g_gla, w_br_a, w_br_b, w_br_c, w_out,
                           ln1_g, ln1_b, ln2_g, ln2_b, w_rg, b_rg, w_re, b_re, w_e_gate, w_e_up, w_e_down)
        xp_f, xp_bf, *rest_p = _prompt_layer(p, xp_f, xp_bf, bsz, t_len)
        xs_f, xs_bf, *rest_s = _sample_layer(p, l, xs_f, xs_bf, cache_kt, cache_vt, state_conv[l],
                                             state_gla[l], page_table)
        outs_p.append(rest_p)
        outs_s.append(rest_s)
    stack = lambda outs, i: jnp.stack([o[i] for o in outs])
    return (xp_f.reshape(bsz, t_len, D_MODEL), xs_f.reshape(dec_b, 1, D_MODEL),
            stack(outs_p, 0), stack(outs_p, 1), stack(outs_p, 2), stack(outs_p, 3).astype(state_gla.dtype),
            stack(outs_s, 0), stack(outs_s, 1), stack(outs_s, 2), stack(outs_s, 3).astype(state_gla.dtype))
```

```python
import functools

import numpy as np
import jax
import jax.numpy as jnp
from jax import lax
from jax.experimental import pallas as pl
from jax.experimental.pallas import tpu as pltpu

D_MODEL = 1024
DEPTH = 4
H_A = 8
DH_A = 64
W_A = H_A * DH_A
MOBA_BLOCK = 256
MOBA_TOPK = 3
CONV_W = 3
W_B = 512
H_C = 4
DK_C = 64
DV_C = 128
QK_C = H_C * DK_C
V_C = H_C * DV_C
FG_RANK = 16
GATE_TEMP = 16.0
N_BRANCH = 3
N_GROUPS = 4
EXPERTS_PER_GROUP = 4
N_EXPERTS = N_GROUPS * EXPERTS_PER_GROUP
D_EXPERT = 512
ALPHA = (2 * DEPTH) ** 0.25
LN_EPS = 1e-5
NEG_INF = -1e30
LOG2E = 1.4426950408889634

LANES = 128
SUBLANES = 8
MIB = 1024 * 1024

GLA_CHUNK = 64
GLA_LEVELS = 6
GLA_TB = 512

_P_Q, _P_K, _P_V = 0, W_A, 2 * W_A
_P_B, _P_C, _P_H = 3 * W_A, 3 * W_A + W_B, 3 * W_A + 2 * W_B
_P_QC = 3 * W_A + 3 * W_B
_P_KC = _P_QC + QK_C
_P_VC = _P_KC + QK_C
_P_RC = _P_VC + V_C
_P_FL = _P_RC + V_C
_P_TOT = _P_FL + LANES

bf16 = jnp.bfloat16
f32 = jnp.float32


def _cparams(sem, vmem_mib):
    return pltpu.CompilerParams(dimension_semantics=sem, vmem_limit_bytes=vmem_mib * MIB)


def _dot(a, b):
    return jnp.dot(a, b, preferred_element_type=f32)


def _dot_nt(a, b):
    return lax.dot_general(a, b, (((1,), (1,)), ((), ())), preferred_element_type=f32)


def _layer_norm_rows(x, g, b):
    mu = jnp.mean(x, axis=-1, keepdims=True)
    xc = x - mu
    var = jnp.mean(xc * xc, axis=-1, keepdims=True)
    return xc * lax.rsqrt(var + LN_EPS) * g + b


def _silu(x):
    return x * jax.nn.sigmoid(x)


def _proj_kernel(seq_tiles, *refs):
    if seq_tiles:
        (x_ref, w_ref, wfg_ref, bfg_ref, cw_ref,
         q_ref, k_ref, v_ref, yb_ref, u_ref, qk_ref, vc_ref, rc_ref, la_ref,
         kb_ref, vb_ref, km_ref, carry_ref) = refs
    else:
        (x_ref, w_ref, wfg_ref, bfg_ref, cw_ref, p0_ref, p1_ref,
         q_ref, k_ref, v_ref, yb_ref, u_ref, qk_ref, vc_ref, rc_ref, la_ref) = refs
    x = x_ref[...]
    tm = x.shape[0]

    def cols(a, n):
        return _dot(x, w_ref[:, a:a + n])

    q = cols(_P_Q, W_A) * (DH_A ** -0.5 * LOG2E)
    k = cols(_P_K, W_A)
    v = cols(_P_V, W_A)
    k_ref[...] = k
    v_ref[...] = v
    if seq_tiles:
        q_ref[0] = jnp.transpose(q).astype(bf16)
        vb_ref[0] = jnp.transpose(v).astype(bf16)
        kb_ref[...] = k.astype(bf16)
        for j in range(tm // MOBA_BLOCK):
            km_ref[0, j:j + 1, :] = jnp.mean(k[j * MOBA_BLOCK:(j + 1) * MOBA_BLOCK], axis=0, keepdims=True)
    else:
        q_ref[...] = q.astype(bf16)

    u = cols(_P_C, W_B) * cols(_P_H, W_B)
    cw = cw_ref[...]
    if seq_tiles:
        i = pl.program_id(0)

        @pl.when(i % seq_tiles == 0)
        def _():
            carry_ref[...] = jnp.zeros_like(carry_ref)

        c6 = carry_ref[SUBLANES - 2:SUBLANES - 1, :]
        c7 = carry_ref[SUBLANES - 1:SUBLANES, :]
        row = lax.broadcasted_iota(jnp.int32, u.shape, 0)
        u1 = jnp.where(row >= 1, pltpu.roll(u, 1, 0), c7)
        u2 = jnp.where(row >= 2, pltpu.roll(u, 2, 0), jnp.where(row == 1, c7, c6))
        tail = u[tm - SUBLANES:, :]
        carry_ref[...] = tail
        u_ref[...] = tail
    else:
        u2 = p0_ref[...]
        u1 = p1_ref[...]
        u_ref[...] = u
    conv = cw[0:1, :] * u2 + cw[1:2, :] * u1 + cw[2:3, :] * u
    yb_ref[...] = (cols(_P_B, W_B) * conv).astype(yb_ref.dtype)

    qk_ref[:, 0:QK_C] = cols(_P_QC, QK_C) * (DK_C ** -0.5)
    qk_ref[:, QK_C:2 * QK_C] = cols(_P_KC, QK_C)
    vc_ref[...] = cols(_P_VC, V_C)
    rc_ref[...] = cols(_P_RC, V_C)
    fl = cols(_P_FL, LANES)
    f = _dot(fl.astype(bf16), wfg_ref[...]) + bfg_ref[...]
    la_ref[...] = (jnp.minimum(f, 0.0) - jnp.log1p(jnp.exp(-jnp.abs(f)))) * (1.0 / GATE_TEMP)


def _proj(x_bf, w_proj, wfg, bfg, conv_w, tm, seq_tiles, prev=None):
    m = x_bf.shape[0]
    n_tiles = m // tm
    row = lambda n: pl.BlockSpec((tm, n), lambda i: (i, 0))
    full = lambda a: pl.BlockSpec(a.shape, lambda i: (0,) * a.ndim)
    in_specs = [row(D_MODEL), full(w_proj), full(wfg), full(bfg), full(conv_w)]
    args = [x_bf, w_proj, wfg, bfg, conv_w]
    scratch = []
    if seq_tiles:
        u_shape, u_spec = (n_tiles * SUBLANES, W_B), pl.BlockSpec((SUBLANES, W_B), lambda i: (i, 0))
        scratch = [pltpu.VMEM((SUBLANES, W_B), f32)]
    else:
        u_shape, u_spec = (m, W_B), row(W_B)
        in_specs += [row(W_B), row(W_B)]
        args += [prev[0], prev[1]]
    tile_t = pl.BlockSpec((1, W_A, tm), lambda i: (i, 0, 0))
    out_shape = [
        jax.ShapeDtypeStruct((n_tiles, W_A, tm) if seq_tiles else (m, W_A), bf16),
        jax.ShapeDtypeStruct((m, W_A), f32),
        jax.ShapeDtypeStruct((m, W_A), f32),
        jax.ShapeDtypeStruct((m, W_B), bf16),
        jax.ShapeDtypeStruct(u_shape, f32),
        jax.ShapeDtypeStruct((m, 2 * QK_C), f32),
        jax.ShapeDtypeStruct((m, V_C), f32),
        jax.ShapeDtypeStruct((m, V_C), f32),
        jax.ShapeDtypeStruct((m, QK_C), f32),
    ]
    out_specs = [tile_t if seq_tiles else row(W_A), row(W_A), row(W_A), row(W_B), u_spec,
                 row(2 * QK_C), row(V_C), row(V_C), row(QK_C)]
    if seq_tiles:
        blocks = tm // MOBA_BLOCK
        out_shape += [jax.ShapeDtypeStruct((m, W_A), bf16),
                      jax.ShapeDtypeStruct((n_tiles, W_A, tm), bf16),
                      jax.ShapeDtypeStruct((n_tiles, blocks, W_A), f32)]
        out_specs += [row(W_A), tile_t, pl.BlockSpec((1, blocks, W_A), lambda i: (i, 0, 0))]
    return pl.pallas_call(
        functools.partial(_proj_kernel, seq_tiles),
        grid=(n_tiles,),
        in_specs=in_specs,
        out_specs=out_specs,
        out_shape=out_shape,
        scratch_shapes=scratch,
        compiler_params=_cparams(("arbitrary",), 56),
        name="proj",
    )(*args)


def _attn_prompt_kernel(qt_ref, k_ref, vt_ref, kmean_ref, o_ref, sel_ref, s_ref):
    qi = pl.program_id(2)
    nb = kmean_ref.shape[1]
    bl = MOBA_BLOCK
    n_pairs = (qi + 1) // 2
    own_slot = s_ref.shape[1] - 1

    qt = qt_ref[0]
    feat = lax.broadcasted_iota(jnp.int32, qt.shape, 0)
    head_lo = feat < DH_A
    qth = (jnp.where(head_lo, qt, jnp.zeros_like(qt)), jnp.where(head_lo, jnp.zeros_like(qt), qt))
    km = kmean_ref[0].astype(bf16)
    blk = lax.broadcasted_iota(jnp.int32, (nb, bl), 0)
    n_valid = jnp.minimum(qi, MOBA_TOPK)

    key = lax.broadcasted_iota(jnp.int32, (bl, bl), 0)
    qry = lax.broadcasted_iota(jnp.int32, (bl, bl), 1)
    own0 = pl.multiple_of(qi * bl, bl)
    k_own = k_ref[pl.ds(own0, bl), :]
    vt_pair = vt_ref[qi // 2]
    vt_own = jnp.where(qi % 2 == 0, vt_pair[:, 0:bl], vt_pair[:, bl:2 * bl])
    m_own = []
    for hh in range(2):
        s = jnp.where(blk < qi, _dot(km, qth[hh]), NEG_INF)
        chosen = jnp.zeros(s.shape, f32)
        for r in range(MOBA_TOPK):
            mx = jnp.max(s, axis=0, keepdims=True)
            first = jnp.min(jnp.where(s == mx, blk, nb), axis=0, keepdims=True)
            pick = blk == first
            chosen = jnp.where(pick, jnp.where(r < n_valid, 1.0, 0.0), chosen)
            s = jnp.where(pick, -jnp.inf, s)
        sel_ref[hh] = chosen
        so = jnp.where(key <= qry, _dot(k_own, qth[hh]), NEG_INF)
        s_ref[hh, own_slot, 0:bl, :] = so
        m_own.append(jnp.max(so, axis=0, keepdims=True))

    def scores(j2, ms):
        j0 = pl.multiple_of(j2 * 2 * bl, 2 * bl)
        kj = k_ref[pl.ds(j0, 2 * bl), :]
        out = []
        for hh in range(2):
            s = _dot(kj, qth[hh])
            m = ms[hh]
            for t in range(2):
                picked = sel_ref[hh, pl.ds(2 * j2 + t, 1), :] > 0.0
                st = jnp.where(picked, s[t * bl:(t + 1) * bl], NEG_INF)
                s_ref[hh, j2, t * bl:(t + 1) * bl, :] = st
                m = jnp.maximum(m, jnp.max(st, axis=0, keepdims=True))
            out.append(m)
        return tuple(out)

    ms = lax.fori_loop(0, n_pairs, scores, tuple(m_own))

    l0, acc0 = [], []
    for hh in range(2):
        p = jnp.exp2(s_ref[hh, own_slot, 0:bl, :] - ms[hh])
        l0.append(jnp.sum(p, axis=0, keepdims=True))
        acc0.append(_dot(vt_own, p.astype(bf16)))

    def values(j2, carry):
        vtj = vt_ref[j2]
        ls, accs = carry
        new_l, new_acc = [], []
        for hh in range(2):
            p = jnp.exp2(s_ref[hh, j2] - ms[hh])
            new_l.append(ls[hh] + jnp.sum(p, axis=0, keepdims=True))
            new_acc.append(accs[hh] + _dot(vtj, p.astype(bf16)))
        return tuple(new_l), tuple(new_acc)

    ls, accs = lax.fori_loop(0, n_pairs, values, (tuple(l0), tuple(acc0)))
    out_t = jnp.where(head_lo, accs[0] / ls[0], accs[1] / ls[1])
    o_ref[...] = jnp.transpose(out_t).astype(o_ref.dtype)


def _attn_prompt(qt, k, vt, kmean, bsz, t_len):
    nb = t_len // MOBA_BLOCK
    bl = MOBA_BLOCK
    pair = 2 * DH_A
    tile = qt.shape[2]
    assert tile == 2 * bl and t_len % tile == 0
    n_tiles = t_len // tile
    slots = n_tiles + 1
    return pl.pallas_call(
        _attn_prompt_kernel,
        grid=(bsz, W_A // pair, nb),
        in_specs=[pl.BlockSpec((1, pair, bl), lambda b, hp, qi: (b * n_tiles + qi // 2, hp, qi % 2)),
                  pl.BlockSpec((t_len, pair), lambda b, hp, qi: (b, hp)),
                  pl.BlockSpec((n_tiles, pair, tile), lambda b, hp, qi: (b, hp, 0)),
                  pl.BlockSpec((1, nb, pair), lambda b, hp, qi: (b, 0, hp))],
        out_specs=pl.BlockSpec((bl, pair), lambda b, hp, qi: (b * nb + qi, hp)),
        out_shape=jax.ShapeDtypeStruct((bsz * t_len, W_A), bf16),
        scratch_shapes=[
            pltpu.VMEM((2, nb, bl), f32),
            pltpu.VMEM((2, slots, tile, bl), f32),
        ],
        compiler_params=_cparams(("arbitrary", "arbitrary", "arbitrary"), 48),
        name="attn_prompt",
    )(qt, k, vt, kmean)


def _attn_select_kernel(layer, pt_ref, q_ref, cache_ref, idx_ref, kbuf_ref, sem_ref):
    b = pl.program_id(0)
    n_pages, page = kbuf_ref.shape[0], kbuf_ref.shape[3]
    per_block = MOBA_BLOCK // page
    nbk = n_pages // per_block

    def page_copy(p):
        return pltpu.make_async_copy(cache_ref.at[layer, pt_ref[b, p]], kbuf_ref.at[p], sem_ref.at[0])

    for p in range(n_pages):
        page_copy(p).start()
    for p in range(n_pages):
        page_copy(p).wait()

    qb = jnp.broadcast_to(q_ref[0], (H_A, DH_A, page))
    scores = []
    for j in range(nbk):
        tok = jnp.sum(kbuf_ref[j * per_block] * qb, axis=1)
        for t in range(1, per_block):
            tok = tok + jnp.sum(kbuf_ref[j * per_block + t] * qb, axis=1)
        scores.append(jnp.sum(tok, axis=1, keepdims=True) * (1.0 / MOBA_BLOCK))

    idx_ref[0] = jnp.zeros(idx_ref.shape[1:], jnp.int32)
    for r in range(MOBA_TOPK):
        mx = functools.reduce(jnp.maximum, scores)
        first = jnp.full(mx.shape, nbk, jnp.int32)
        for j in reversed(range(nbk)):
            first = jnp.where(scores[j] == mx, j, first)
        idx_ref[0, :, r:r + 1] = first
        scores = [jnp.where(first == j, -jnp.inf, s) for j, s in enumerate(scores)]


def _attn_select(page_table, q_col, cache_kt, layer):
    bsz, n_pages = page_table.shape
    page = cache_kt.shape[4]
    grid_spec = pltpu.PrefetchScalarGridSpec(
        num_scalar_prefetch=1,
        grid=(bsz,),
        in_specs=[
            pl.BlockSpec((1, H_A, DH_A, 1), lambda b, pt: (b, 0, 0, 0)),
            pl.BlockSpec(memory_space=pl.ANY),
        ],
        out_specs=pl.BlockSpec((1, H_A, LANES), lambda b, pt: (b, 0, 0)),
        scratch_shapes=[pltpu.VMEM((n_pages, H_A, DH_A, page), f32), pltpu.SemaphoreType.DMA((1,))],
    )
    return pl.pallas_call(
        functools.partial(_attn_select_kernel, layer),
        grid_spec=grid_spec,
        out_shape=jax.ShapeDtypeStruct((bsz, H_A, LANES), jnp.int32),
        compiler_params=_cparams(("arbitrary",), 40),
        name="attn_select",
    )(page_table, q_col, cache_kt)


def _attn_sample_kernel(layer, pt_ref, sel_ref, q_ref, kn_ref, vn_ref, ck_ref, cv_ref, o_ref,
                        kbuf_ref, vbuf_ref, sem_ref):
    b = pl.program_id(0)
    page = ck_ref.shape[4]
    per_block = MOBA_BLOCK // page

    def copies(h, r, t):
        pg = pt_ref[b, sel_ref[b, h * MOBA_TOPK + r] * per_block + t]
        dst = pl.ds((r * per_block + t) * page, page)
        return (pltpu.make_async_copy(ck_ref.at[layer, pg, h], kbuf_ref.at[h, :, dst], sem_ref.at[0]),
                pltpu.make_async_copy(cv_ref.at[layer, pg, h], vbuf_ref.at[h, :, dst], sem_ref.at[1]))

    slots = [(h, r, t) for h in range(H_A) for r in range(MOBA_TOPK) for t in range(per_block)]
    for s in slots:
        ck, cv = copies(*s)
        ck.start()
        cv.start()
    for s in slots:
        ck, cv = copies(*s)
        ck.wait()
        cv.wait()

    for h in range(H_A):
        qh = q_ref[0, h:h + 1, :]
        kh = kbuf_ref[h].astype(bf16)
        vh = vbuf_ref[h].astype(bf16)
        q8 = jnp.broadcast_to(qh, (SUBLANES, DH_A)).astype(bf16)
        s = _dot(q8, kh)
        s_own = jnp.sum(qh * kn_ref[0, h:h + 1, :], axis=1, keepdims=True)
        m = jnp.maximum(jnp.max(s, axis=1, keepdims=True), s_own)
        p = jnp.exp2(s - m)
        p_own = jnp.exp2(s_own - m)
        den = jnp.sum(p, axis=1, keepdims=True) + p_own
        out = (_dot_nt(p.astype(bf16), vh) + p_own * vn_ref[0, h:h + 1, :]) / den
        o_ref[0, h:h + 1, :] = out[0:1, :].astype(o_ref.dtype)


def _attn_sample(page_table, sel, q, k_new, v_new, cache_k, cache_v, layer):
    bsz = page_table.shape[0]
    n_sel = MOBA_TOPK * MOBA_BLOCK
    tok = pl.BlockSpec((1, H_A, DH_A), lambda b, pt, s: (b, 0, 0))
    grid_spec = pltpu.PrefetchScalarGridSpec(
        num_scalar_prefetch=2,
        grid=(bsz,),
        in_specs=[tok, tok, tok, pl.BlockSpec(memory_space=pl.ANY), pl.BlockSpec(memory_space=pl.ANY)],
        out_specs=tok,
        scratch_shapes=[pltpu.VMEM((H_A, DH_A, n_sel), f32), pltpu.VMEM((H_A, DH_A, n_sel), f32),
                        pltpu.SemaphoreType.DMA((2,))],
    )
    return pl.pallas_call(
        functools.partial(_attn_sample_kernel, layer),
        grid_spec=grid_spec,
        out_shape=jax.ShapeDtypeStruct((bsz, H_A, DH_A), bf16),
        compiler_params=_cparams(("arbitrary",), 32),
        name="attn_sample",
    )(page_table, sel, q, k_new, v_new, cache_k, cache_v)


def _gla_tables():
    c = GLA_CHUNK
    tri = np.tril(np.ones((c, c), np.float32))
    mats = [tri]
    masks = []
    t = np.arange(c)
    for lev in range(GLA_LEVELS):
        h = 1 << lev
        mid = (t // (2 * h)) * (2 * h) + h
        mats.append(tri[mid - 1])
        same = (t[:, None] // (2 * h)) == (t[None, :] // (2 * h))
        masks.append(same & ((t[:, None] & h) != 0) & ((t[None, :] & h) == 0))
    masks.append(np.eye(c, dtype=bool))
    sel = np.concatenate(mats, axis=0)
    lmask = np.stack([np.tile(mk, (H_C, 1)) for mk in masks]).astype(np.float32)
    return sel, lmask


def _gla_prompt_kernel(qk_ref, la_ref, v_ref, r_ref, g_ref, s0_ref, sel_ref, lmask_ref,
                       y_ref, sout_ref, s_ref):
    c = GLA_CHUNK

    @pl.when(pl.program_id(1) == 0)
    def _():
        s_ref[...] = s0_ref[0]

    sel = sel_ref[...]
    g = g_ref[...]
    lane = lax.broadcasted_iota(jnp.int32, (c, QK_C), 1)
    rowc = lax.broadcasted_iota(jnp.int32, (c, QK_C), 0)
    head_of_lane = lane // DK_C
    head_of_row = lax.broadcasted_iota(jnp.int32, (QK_C, DV_C), 0) // DK_C

    def stack_heads(a):
        return jnp.concatenate([jnp.where(head_of_lane == h, a, 0.0) for h in range(H_C)], axis=0)

    def chunk(ci, carry):
        r0 = pl.multiple_of(ci * c, c)
        la = la_ref[pl.ds(r0, c), :]
        la1 = la.astype(bf16)
        rem = la - la1.astype(f32)
        la2 = rem.astype(bf16)
        la3 = (rem - la2.astype(f32)).astype(bf16)
        ball = _dot(sel, la1) + _dot(sel, la2) + _dot(sel, la3)
        b = ball[0:c]
        q = qk_ref[pl.ds(r0, c), 0:QK_C]
        k = qk_ref[pl.ds(r0, c), QK_C:2 * QK_C]
        v = v_ref[pl.ds(r0, c), :].astype(bf16)

        att = lmask_ref[GLA_LEVELS] * _dot_nt(stack_heads(q).astype(bf16), k.astype(bf16))
        for lev in range(GLA_LEVELS):
            ref_b = ball[(lev + 1) * c:(lev + 2) * c]
            right = (rowc & (1 << lev)) != 0
            qt = q * jnp.exp(jnp.where(right, b - ref_b, NEG_INF))
            kt = k * jnp.exp(jnp.where(right, NEG_INF, ref_b - b))
            att = att + lmask_ref[lev] * _dot_nt(stack_heads(qt).astype(bf16), kt.astype(bf16))

        s_old = s_ref[...]
        inter = _dot(stack_heads(q * jnp.exp(b)).astype(bf16), s_old.astype(bf16))
        b_last = b[c - 1:c, :]
        kt_t = jnp.transpose(k * jnp.exp(b_last - b))
        b_t = jnp.transpose(b)
        upd = _dot(kt_t.astype(bf16), v)
        s_new = jnp.exp(b_t[:, c - 1:c]) * s_old
        for h in range(H_C):
            s_new = s_new + jnp.where(head_of_row == h, upd[:, h * DV_C:(h + 1) * DV_C], 0.0)
        s_ref[...] = s_new

        att_bf = att.astype(bf16)
        for h in range(H_C):
            o = (_dot(att_bf[h * c:(h + 1) * c], v[:, h * DV_C:(h + 1) * DV_C])
                 + inter[h * c:(h + 1) * c])
            mu = jnp.mean(o, axis=-1, keepdims=True)
            oc = o - mu
            var = jnp.mean(oc * oc, axis=-1, keepdims=True)
            hs = slice(h * DV_C, (h + 1) * DV_C)
            gate = g[:, hs] * _silu(r_ref[pl.ds(r0, c), hs])
            y_ref[pl.ds(r0, c), hs] = (oc * lax.rsqrt(var + LN_EPS) * gate).astype(y_ref.dtype)
        return carry

    lax.fori_loop(0, qk_ref.shape[0] // c, chunk, 0)
    sout_ref[0] = s_ref[...]


def _gla_prompt(qk, la, v, r, g_gla, s0, bsz, t_len):
    tb = min(GLA_TB, t_len)
    nt = t_len // tb
    sel_np, lmask_np = _gla_tables()
    sel = jnp.asarray(sel_np, bf16)
    lmask = jnp.asarray(lmask_np, f32)
    row = lambda n: pl.BlockSpec((tb, n), lambda b, t: (b * nt + t, 0))
    st = pl.BlockSpec((1, QK_C, DV_C), lambda b, t: (b, 0, 0))
    return pl.pallas_call(
        _gla_prompt_kernel,
        grid=(bsz, nt),
        in_specs=[row(2 * QK_C), row(QK_C), row(V_C), row(V_C),
                  pl.BlockSpec((1, V_C), lambda b, t: (0, 0)), st,
                  pl.BlockSpec(sel.shape, lambda b, t: (0, 0)),
                  pl.BlockSpec(lmask.shape, lambda b, t: (0, 0, 0))],
        out_specs=[row(V_C), st],
        out_shape=[jax.ShapeDtypeStruct((bsz * t_len, V_C), bf16),
                   jax.ShapeDtypeStruct((bsz, QK_C, DV_C), f32)],
        scratch_shapes=[pltpu.VMEM((QK_C, DV_C), f32)],
        compiler_params=_cparams(("arbitrary", "arbitrary"), 40),
        name="gla_prompt",
    )(qk, la, v, r, g_gla, s0, sel, lmask)


def _gla_step_kernel(q_ref, k_ref, la_ref, v_ref, r_ref, g_ref, s_ref, y_ref, sout_ref):
    head_of_row = lax.broadcasted_iota(jnp.int32, (QK_C, DV_C), 0) // DK_C
    v = v_ref[0]
    v_rows = jnp.zeros((QK_C, DV_C), f32)
    for h in range(H_C):
        v_rows = jnp.where(head_of_row == h, v[:, h * DV_C:(h + 1) * DV_C], v_rows)
    s_new = jnp.exp(la_ref[0]) * s_ref[0] + k_ref[0] * v_rows
    sout_ref[0] = s_new
    qs = q_ref[0] * s_new
    g = g_ref[...]
    for h in range(H_C):
        o = jnp.sum(qs[h * DK_C:(h + 1) * DK_C], axis=0, keepdims=True)
        mu = jnp.mean(o, axis=-1, keepdims=True)
        oc = o - mu
        var = jnp.mean(oc * oc, axis=-1, keepdims=True)
        hs = slice(h * DV_C, (h + 1) * DV_C)
        y_ref[0, :, hs] = (oc * lax.rsqrt(var + LN_EPS) * g[:, hs] * _silu(r_ref[0, :, hs])).astype(y_ref.dtype)


def _gla_step(q_col, k_col, la_col, v, r, g_gla, s):
    bsz = s.shape[0]
    col = pl.BlockSpec((1, QK_C, 1), lambda b: (b, 0, 0))
    tok = pl.BlockSpec((1, 1, V_C), lambda b: (b, 0, 0))
    st = pl.BlockSpec((1, QK_C, DV_C), lambda b: (b, 0, 0))
    return pl.pallas_call(
        _gla_step_kernel,
        grid=(bsz,),
        in_specs=[col, col, col, tok, tok, pl.BlockSpec((1, V_C), lambda b: (0, 0)), st],
        out_specs=[tok, st],
        out_shape=[jax.ShapeDtypeStruct((bsz, 1, V_C), bf16), jax.ShapeDtypeStruct(s.shape, f32)],
        compiler_params=_cparams(("arbitrary",), 16),
        name="gla_step",
    )(q_col, k_col, la_col, v, r, g_gla, s)


def _merge_kernel(xb_ref, xf_ref, ya_ref, yb_ref, yc_ref, wg_ref, wa_ref, wb_ref, wc_ref, wo_ref,
                  g_ref, b_ref, hf_ref, hb_ref):
    xb = xb_ref[...]
    merged = None
    for i, (y_ref, w_ref) in enumerate(((ya_ref, wa_ref), (yb_ref, wb_ref), (yc_ref, wc_ref))):
        gate = jax.nn.sigmoid(_dot(xb, wg_ref[:, i * D_MODEL:(i + 1) * D_MODEL]))
        term = gate * _dot(y_ref[...], w_ref[...])
        merged = term if merged is None else merged + term
    mix = _dot(merged.astype(bf16), wo_ref[...])
    h = _layer_norm_rows(ALPHA * xf_ref[...] + mix, g_ref[...], b_ref[...])
    hf_ref[...] = h
    hb_ref[...] = h.astype(bf16)


def _merge(x_bf, x_f, y_a, y_b, y_c, w_gate, w_a, w_b, w_c, w_o, ln_g, ln_b, tm):
    m = x_bf.shape[0]
    row = lambda n: pl.BlockSpec((tm, n), lambda i: (i, 0))
    full = lambda a: pl.BlockSpec(a.shape, lambda i: (0,) * a.ndim)
    return pl.pallas_call(
        _merge_kernel,
        grid=(m // tm,),
        in_specs=[row(D_MODEL), row(D_MODEL), row(W_A), row(W_B), row(V_C),
                  full(w_gate), full(w_a), full(w_b), full(w_c), full(w_o), full(ln_g), full(ln_b)],
        out_specs=[row(D_MODEL), row(D_MODEL)],
        out_shape=[jax.ShapeDtypeStruct((m, D_MODEL), f32), jax.ShapeDtypeStruct((m, D_MODEL), bf16)],
        compiler_params=_cparams(("arbitrary",), 56),
        name="merge",
    )(x_bf, x_f, y_a, y_b, y_c, w_gate, w_a, w_b, w_c, w_o, ln_g, ln_b)


def _group_probs(logits):
    lane = lax.broadcasted_iota(jnp.int32, logits.shape, 1)
    gl = jnp.where(lane < N_GROUPS, logits, -jnp.inf)
    pg = jnp.exp(gl - jnp.max(gl, axis=1, keepdims=True))
    pg = pg / jnp.sum(pg, axis=1, keepdims=True)
    p_top = jnp.max(pg, axis=1, keepdims=True)
    g_idx = jnp.min(jnp.where(pg == p_top, lane, LANES), axis=1, keepdims=True)
    return pg, g_idx


def _combine_weights(logits, pg, g_idx):
    lane = lax.broadcasted_iota(jnp.int32, logits.shape, 1)
    p_top = jnp.sum(jnp.where(lane == g_idx, pg, 0.0), axis=1, keepdims=True)
    e0 = N_GROUPS + g_idx * EXPERTS_PER_GROUP
    el = jnp.where((lane >= e0) & (lane < e0 + EXPERTS_PER_GROUP), logits, -jnp.inf)
    pe = jnp.exp(el - jnp.max(el, axis=1, keepdims=True))
    pe = pe / jnp.sum(pe, axis=1, keepdims=True)
    p1 = jnp.max(pe, axis=1, keepdims=True)
    i1 = jnp.min(jnp.where(pe == p1, lane, LANES), axis=1, keepdims=True)
    rest = jnp.where(lane == i1, -1.0, pe)
    p2 = jnp.max(rest, axis=1, keepdims=True)
    i2 = jnp.min(jnp.where(rest == p2, lane, LANES), axis=1, keepdims=True)
    den = p1 + p2
    return jnp.where(lane == i1, p_top * p1 / den, jnp.where(lane == i2, p_top * p2 / den, 0.0))


def _moe_kernel(hb_ref, hf_ref, wr_ref, br_ref, wg_ref, wu_ref, wd_ref, g_ref, b_ref,
                yf_ref, yb_ref, comb_ref, acc_ref):
    e = pl.program_id(1)
    hb = hb_ref[...]

    @pl.when(e == 0)
    def _():
        logits = _dot(hb, wr_ref[...]) + br_ref[...]
        pg, g_idx = _group_probs(logits)
        comb_ref[...] = _combine_weights(logits, pg, g_idx)
        acc_ref[...] = jnp.zeros_like(acc_ref)

    comb = comb_ref[...]
    lane = lax.broadcasted_iota(jnp.int32, comb.shape, 1)
    cw = jnp.sum(jnp.where(lane == N_GROUPS + e, comb, 0.0), axis=1, keepdims=True)
    act = _silu(_dot(hb, wg_ref[0, 0])) * _dot(hb, wu_ref[0, 0]) * cw
    acc_ref[...] += _dot(act.astype(bf16), wd_ref[0, 0])

    @pl.when(e == N_EXPERTS - 1)
    def _():
        y = _layer_norm_rows(ALPHA * hf_ref[...] + acc_ref[...], g_ref[...], b_ref[...])
        yf_ref[...] = y
        yb_ref[...] = y.astype(bf16)


def _cast_kernel(x_ref, o_ref):
    o_ref[...] = x_ref[...].astype(o_ref.dtype)


def _cast_bf16(w):
    rows, cols = int(np.prod(w.shape[:-1])), w.shape[-1]
    br = (4 * MIB) // (4 * cols)
    assert rows % br == 0
    spec = pl.BlockSpec((br, cols), lambda i: (i, 0))
    out = pl.pallas_call(
        _cast_kernel,
        grid=(rows // br,),
        in_specs=[spec],
        out_specs=spec,
        out_shape=jax.ShapeDtypeStruct((rows, cols), bf16),
        compiler_params=_cparams(("arbitrary",), 32),
        name="cast_bf16",
    )(w.reshape(rows, cols))
    return out.reshape(w.shape)


def _moe(h_bf, h_f, w_router, b_router, w_gate, w_up, w_down, ln_g, ln_b, tm, layer):
    m = h_bf.shape[0]
    row = lambda n: pl.BlockSpec((tm, n), lambda i, e: (i, 0))
    full = lambda a: pl.BlockSpec(a.shape, lambda i, e: (0,) * a.ndim)
    return pl.pallas_call(
        _moe_kernel,
        grid=(m // tm, N_EXPERTS),
        in_specs=[row(D_MODEL), row(D_MODEL), full(w_router), full(b_router),
                  pl.BlockSpec((1, 1, D_MODEL, D_EXPERT), lambda i, e: (layer, e, 0, 0)),
                  pl.BlockSpec((1, 1, D_MODEL, D_EXPERT), lambda i, e: (layer, e, 0, 0)),
                  pl.BlockSpec((1, 1, D_EXPERT, D_MODEL), lambda i, e: (layer, e, 0, 0)),
                  full(ln_g), full(ln_b)],
        out_specs=[row(D_MODEL), row(D_MODEL)],
        out_shape=[jax.ShapeDtypeStruct((m, D_MODEL), f32), jax.ShapeDtypeStruct((m, D_MODEL), bf16)],
        scratch_shapes=[pltpu.VMEM((tm, LANES), f32), pltpu.VMEM((tm, D_MODEL), f32)],
        compiler_params=_cparams(("arbitrary", "arbitrary"), 56),
        name="moe",
    )(h_bf, h_f, w_router, b_router, w_gate, w_up, w_down, ln_g, ln_b)


def _row_tile(m):
    for tm in (512, 256, 128, 64, 32, 16, 8):
        if m % tm == 0:
            return tm
    raise ValueError(f"row count {m} must be a multiple of {SUBLANES}")


def _layer_weights(l, w_in, conv_w, w_fg2, b_fg, g_gla, w_br_a, w_br_b, w_br_c, w_out,
                   ln1_g, ln1_b, ln2_g, ln2_b, w_rg, b_rg, w_re, b_re):
    wi = w_in[l]
    n_front = _P_FL
    gate0 = n_front + FG_RANK
    w_proj = jnp.concatenate(
        [wi[:, :n_front], jnp.pad(wi[:, n_front:gate0], ((0, 0), (0, LANES - FG_RANK)))], axis=1).astype(bf16)
    pad_r = LANES - N_GROUPS - N_EXPERTS
    return dict(
        w_proj=w_proj,
        w_gate=wi[:, gate0:].astype(bf16),
        wfg=jnp.pad(w_fg2[l], ((0, LANES - FG_RANK), (0, 0))).astype(bf16),
        bfg=b_fg[l][None, :],
        conv_w=conv_w[l],
        g_gla=g_gla[l][None, :],
        w_a=w_br_a[l].astype(bf16), w_b=w_br_b[l].astype(bf16), w_c=w_br_c[l].astype(bf16),
        w_o=w_out[l].astype(bf16),
        ln1_g=ln1_g[l][None, :], ln1_b=ln1_b[l][None, :], ln2_g=ln2_g[l][None, :], ln2_b=ln2_b[l][None, :],
        w_router=jnp.pad(jnp.concatenate([w_rg[l], w_re[l]], axis=1), ((0, 0), (0, pad_r))).astype(bf16),
        b_router=jnp.pad(jnp.concatenate([b_rg[l], b_re[l]]), (0, pad_r))[None, :],
    )


MOE_ROWS = 1024


def _finish_layer(p, experts, layer, x_bf, x_f, y_a, y_b, y_c, tm):
    h_f, h_bf = _merge(x_bf, x_f, y_a, y_b, y_c, p["w_gate"], p["w_a"], p["w_b"], p["w_c"], p["w_o"],
                       p["ln1_g"], p["ln1_b"], tm)
    tm_moe = MOE_ROWS if h_f.shape[0] % MOE_ROWS == 0 else tm
    return _moe(h_bf, h_f, p["w_router"], p["b_router"], *experts, p["ln2_g"], p["ln2_b"], tm_moe, layer)


def _prompt_layer(p, experts, layer, x_f, x_bf, bsz, t_len):
    tm = _row_tile(t_len)
    q, k, v, y_b, u_tail, qk, vc, rc, la, k_bf, v_bf, kmean = _proj(
        x_bf, p["w_proj"], p["wfg"], p["bfg"], p["conv_w"], tm, t_len // tm)
    y_a = _attn_prompt(q, k_bf, v_bf, kmean.reshape(bsz, t_len // MOBA_BLOCK, W_A), bsz, t_len)
    y_c, s_fin = _gla_prompt(qk, la, vc, rc, p["g_gla"], jnp.zeros((bsz, QK_C, DV_C), f32), bsz, t_len)
    x_f, x_bf = _finish_layer(p, experts, layer, x_bf, x_f, y_a, y_b, y_c, tm)
    conv_state = u_tail.reshape(bsz, t_len // tm, SUBLANES, W_B)[:, -1, SUBLANES - (CONV_W - 1):, :]
    return (x_f, x_bf, k.reshape(bsz, t_len, H_A, DH_A), v.reshape(bsz, t_len, H_A, DH_A), conv_state,
            s_fin.reshape(bsz, H_C, DK_C, DV_C))


def _sample_layer(p, experts, layer, x_f, x_bf, cache_kt, cache_vt, conv_prev, gla_prev, page_table):
    bsz = x_f.shape[0]
    q, k, v, y_b, u, qk, vc, rc, la = _proj(
        x_bf, p["w_proj"], p["wfg"], p["bfg"], p["conv_w"], bsz, 0, prev=(conv_prev[:, 0], conv_prev[:, 1]))
    heads = lambda a: a.reshape(bsz, H_A, DH_A)
    q_h = heads(q.astype(f32))
    idx = _attn_select(page_table, q_h[..., None], cache_kt, layer)
    sel = idx[:, :, :MOBA_TOPK].reshape(bsz, H_A * MOBA_TOPK)
    y_a = _attn_sample(page_table, sel, q_h, heads(k), heads(v), cache_kt, cache_vt, layer)
    y_c, s_new = _gla_step(qk[:, :QK_C, None], qk[:, QK_C:, None], la[:, :, None], vc[:, None, :],
                           rc[:, None, :], p["g_gla"], gla_prev.reshape(bsz, QK_C, DV_C))
    x_f, x_bf = _finish_layer(p, experts, layer, x_bf, x_f, y_a.reshape(bsz, W_A), y_b, y_c.reshape(bsz, V_C), bsz)
    conv_state = jnp.stack([conv_prev[:, 1], u], axis=1)
    return (x_f, x_bf, k.reshape(bsz, 1, H_A, DH_A), v.reshape(bsz, 1, H_A, DH_A), conv_state,
            s_new.reshape(bsz, H_C, DK_C, DV_C))


def kernel(x_prompt, x_sample, cache_k, cache_v, state_conv, state_gla, page_table, w_in, conv_w, w_fg2, b_fg, g_gla, w_br_a, w_br_b, w_br_c, w_out, ln1_g, ln1_b, ln2_g, ln2_b, w_rg, b_rg, w_re, b_re, w_e_gate, w_e_up, w_e_down):
    bsz, t_len, _ = x_prompt.shape
    dec_b = x_sample.shape[0]
    assert x_sample.shape[1] == 1 and t_len % MOBA_BLOCK == 0 and dec_b % SUBLANES == 0
    assert (page_table.shape[1] * cache_k.shape[2]) % MOBA_BLOCK == 0 and MOBA_BLOCK % cache_k.shape[2] == 0
    cache_kt = jnp.transpose(cache_k, (0, 1, 3, 4, 2))
    cache_vt = jnp.transpose(cache_v, (0, 1, 3, 4, 2))
    xp_f = x_prompt.reshape(bsz * t_len, D_MODEL)
    xs_f = x_sample.reshape(dec_b, D_MODEL)
    xp_bf, xs_bf = xp_f.astype(bf16), xs_f.astype(bf16)
    experts = (_cast_bf16(w_e_gate), _cast_bf16(w_e_up), _cast_bf16(w_e_down))
    outs_p, outs_s = [], []
    for l in range(DEPTH):
        p = _layer_weights(l, w_in, conv_w, w_fg2, b_fg, g_gla, w_br_a, w_br_b, w_br_c, w_out,
                           ln1_g, ln1_b, ln2_g, ln2_b, w_rg, b_rg, w_re, b_re)
        xp_f, xp_bf, *rest_p = _prompt_layer(p, experts, l, xp_f, xp_bf, bsz, t_len)
        xs_f, xs_bf, *rest_s = _sample_layer(p, experts, l, xs_f, xs_bf, cache_kt, cache_vt, state_conv[l],
                                             state_gla[l], page_table)
        outs_p.append(rest_p)
        outs_s.append(rest_s)
    stack = lambda outs, i: jnp.stack([o[i] for o in outs])
    return (xp_f.reshape(bsz, t_len, D_MODEL), xs_f.reshape(dec_b, 1, D_MODEL),
            stack(outs_p, 0), stack(outs_p, 1), stack(outs_p, 2), stack(outs_p, 3).astype(state_gla.dtype),
            stack(outs_s, 0), stack(outs_s, 1), stack(outs_s, 2), stack(outs_s, 3).astype(state_gla.dtype))
```

```python
import functools

import numpy as np
import jax
import jax.numpy as jnp
from jax import lax
from jax.experimental import pallas as pl
from jax.experimental.pallas import tpu as pltpu

D_MODEL = 1024
DEPTH = 4
H_A = 8
DH_A = 64
W_A = H_A * DH_A
MOBA_BLOCK = 256
MOBA_TOPK = 3
CONV_W = 3
W_B = 512
H_C = 4
DK_C = 64
DV_C = 128
QK_C = H_C * DK_C
V_C = H_C * DV_C
FG_RANK = 16
GATE_TEMP = 16.0
N_BRANCH = 3
N_GROUPS = 4
EXPERTS_PER_GROUP = 4
N_EXPERTS = N_GROUPS * EXPERTS_PER_GROUP
D_EXPERT = 512
ALPHA = (2 * DEPTH) ** 0.25
LN_EPS = 1e-5
NEG_INF = -1e30
LOG2E = 1.4426950408889634

LANES = 128
SUBLANES = 8
MIB = 1024 * 1024

GLA_CHUNK = 64
GLA_LEVELS = 6
GLA_TB = 512

_P_Q, _P_K, _P_V = 0, W_A, 2 * W_A
_P_B, _P_C, _P_H = 3 * W_A, 3 * W_A + W_B, 3 * W_A + 2 * W_B
_P_QC = 3 * W_A + 3 * W_B
_P_KC = _P_QC + QK_C
_P_VC = _P_KC + QK_C
_P_RC = _P_VC + V_C
_P_FL = _P_RC + V_C
_P_TOT = _P_FL + LANES

bf16 = jnp.bfloat16
f32 = jnp.float32


def _cparams(sem, vmem_mib):
    return pltpu.CompilerParams(dimension_semantics=sem, vmem_limit_bytes=vmem_mib * MIB)


def _dot(a, b):
    return jnp.dot(a, b, preferred_element_type=f32)


def _dot_nt(a, b):
    return lax.dot_general(a, b, (((1,), (1,)), ((), ())), preferred_element_type=f32)


def _layer_norm_rows(x, g, b):
    mu = jnp.mean(x, axis=-1, keepdims=True)
    xc = x - mu
    var = jnp.mean(xc * xc, axis=-1, keepdims=True)
    return xc * lax.rsqrt(var + LN_EPS) * g + b


def _silu(x):
    return x * jax.nn.sigmoid(x)


def _proj_kernel(seq_tiles, *refs):
    if seq_tiles:
        (x_ref, w_ref, wfg_ref, bfg_ref, cw_ref,
         q_ref, k_ref, v_ref, yb_ref, u_ref, qk_ref, vc_ref, rc_ref, la_ref,
         kb_ref, vb_ref, km_ref, carry_ref) = refs
    else:
        (x_ref, w_ref, wfg_ref, bfg_ref, cw_ref, p0_ref, p1_ref,
         q_ref, k_ref, v_ref, yb_ref, u_ref, qk_ref, vc_ref, rc_ref, la_ref) = refs
    x = x_ref[...]
    tm = x.shape[0]

    def cols(a, n):
        return _dot(x, w_ref[:, a:a + n])

    q = cols(_P_Q, W_A) * (DH_A ** -0.5 * LOG2E)
    k = cols(_P_K, W_A)
    v = cols(_P_V, W_A)
    k_ref[...] = k
    v_ref[...] = v
    if seq_tiles:
        q_ref[0] = jnp.transpose(q).astype(bf16)
        vb_ref[0] = jnp.transpose(v).astype(bf16)
        kb_ref[...] = k.astype(bf16)
        for j in range(tm // MOBA_BLOCK):
            km_ref[0, j:j + 1, :] = jnp.mean(k[j * MOBA_BLOCK:(j + 1) * MOBA_BLOCK], axis=0, keepdims=True)
    else:
        q_ref[...] = q.astype(bf16)

    u = cols(_P_C, W_B) * cols(_P_H, W_B)
    cw = cw_ref[...]
    if seq_tiles:
        i = pl.program_id(0)

        @pl.when(i % seq_tiles == 0)
        def _():
            carry_ref[...] = jnp.zeros_like(carry_ref)

        c6 = carry_ref[SUBLANES - 2:SUBLANES - 1, :]
        c7 = carry_ref[SUBLANES - 1:SUBLANES, :]
        row = lax.broadcasted_iota(jnp.int32, u.shape, 0)
        u1 = jnp.where(row >= 1, pltpu.roll(u, 1, 0), c7)
        u2 = jnp.where(row >= 2, pltpu.roll(u, 2, 0), jnp.where(row == 1, c7, c6))
        tail = u[tm - SUBLANES:, :]
        carry_ref[...] = tail
        u_ref[...] = tail
    else:
        u2 = p0_ref[...]
        u1 = p1_ref[...]
        u_ref[...] = u
    conv = cw[0:1, :] * u2 + cw[1:2, :] * u1 + cw[2:3, :] * u
    yb_ref[...] = (cols(_P_B, W_B) * conv).astype(yb_ref.dtype)

    qk_ref[:, 0:QK_C] = cols(_P_QC, QK_C) * (DK_C ** -0.5)
    qk_ref[:, QK_C:2 * QK_C] = cols(_P_KC, QK_C)
    vc_ref[...] = cols(_P_VC, V_C)
    rc_ref[...] = cols(_P_RC, V_C)
    fl = cols(_P_FL, LANES)
    f = _dot(fl.astype(bf16), wfg_ref[...]) + bfg_ref[...]
    la_ref[...] = (jnp.minimum(f, 0.0) - jnp.log1p(jnp.exp(-jnp.abs(f)))) * (1.0 / GATE_TEMP)


def _proj(x_bf, w_proj, wfg, bfg, conv_w, tm, seq_tiles, prev=None):
    m = x_bf.shape[0]
    n_tiles = m // tm
    row = lambda n: pl.BlockSpec((tm, n), lambda i: (i, 0))
    full = lambda a: pl.BlockSpec(a.shape, lambda i: (0,) * a.ndim)
    in_specs = [row(D_MODEL), full(w_proj), full(wfg), full(bfg), full(conv_w)]
    args = [x_bf, w_proj, wfg, bfg, conv_w]
    scratch = []
    if seq_tiles:
        u_shape, u_spec = (n_tiles * SUBLANES, W_B), pl.BlockSpec((SUBLANES, W_B), lambda i: (i, 0))
        scratch = [pltpu.VMEM((SUBLANES, W_B), f32)]
    else:
        u_shape, u_spec = (m, W_B), row(W_B)
        in_specs += [row(W_B), row(W_B)]
        args += [prev[0], prev[1]]
    tile_t = pl.BlockSpec((1, W_A, tm), lambda i: (i, 0, 0))
    out_shape = [
        jax.ShapeDtypeStruct((n_tiles, W_A, tm) if seq_tiles else (m, W_A), bf16),
        jax.ShapeDtypeStruct((m, W_A), f32),
        jax.ShapeDtypeStruct((m, W_A), f32),
        jax.ShapeDtypeStruct((m, W_B), bf16),
        jax.ShapeDtypeStruct(u_shape, f32),
        jax.ShapeDtypeStruct((m, 2 * QK_C), f32),
        jax.ShapeDtypeStruct((m, V_C), f32),
        jax.ShapeDtypeStruct((m, V_C), f32),
        jax.ShapeDtypeStruct((m, QK_C), f32),
    ]
    out_specs = [tile_t if seq_tiles else row(W_A), row(W_A), row(W_A), row(W_B), u_spec,
                 row(2 * QK_C), row(V_C), row(V_C), row(QK_C)]
    if seq_tiles:
        blocks = tm // MOBA_BLOCK
        out_shape += [jax.ShapeDtypeStruct((m, W_A), bf16),
                      jax.ShapeDtypeStruct((n_tiles, W_A, tm), bf16),
                      jax.ShapeDtypeStruct((n_tiles, blocks, W_A), f32)]
        out_specs += [row(W_A), tile_t, pl.BlockSpec((1, blocks, W_A), lambda i: (i, 0, 0))]
    return pl.pallas_call(
        functools.partial(_proj_kernel, seq_tiles),
        grid=(n_tiles,),
        in_specs=in_specs,
        out_specs=out_specs,
        out_shape=out_shape,
        scratch_shapes=scratch,
        compiler_params=_cparams(("arbitrary",), 56),
        name="proj",
    )(*args)


def _attn_prompt_kernel(qt_ref, k_ref, vt_ref, kmean_ref, o_ref, sel_ref, s_ref):
    qi = pl.program_id(2)
    nb = kmean_ref.shape[1]
    bl = MOBA_BLOCK
    n_pairs = (qi + 1) // 2
    own_slot = s_ref.shape[1] - 1

    qt = qt_ref[0]
    feat = lax.broadcasted_iota(jnp.int32, qt.shape, 0)
    head_lo = feat < DH_A
    qth = (jnp.where(head_lo, qt, jnp.zeros_like(qt)), jnp.where(head_lo, jnp.zeros_like(qt), qt))
    km = kmean_ref[0].astype(bf16)
    blk = lax.broadcasted_iota(jnp.int32, (nb, bl), 0)
    n_valid = jnp.minimum(qi, MOBA_TOPK)

    key = lax.broadcasted_iota(jnp.int32, (bl, bl), 0)
    qry = lax.broadcasted_iota(jnp.int32, (bl, bl), 1)
    own0 = pl.multiple_of(qi * bl, bl)
    k_own = k_ref[pl.ds(own0, bl), :]
    vt_pair = vt_ref[qi // 2]
    vt_own = jnp.where(qi % 2 == 0, vt_pair[:, 0:bl], vt_pair[:, bl:2 * bl])
    m_own = []
    for hh in range(2):
        s = jnp.where(blk < qi, _dot(km, qth[hh]), NEG_INF)
        chosen = jnp.zeros(s.shape, f32)
        for r in range(MOBA_TOPK):
            mx = jnp.max(s, axis=0, keepdims=True)
            first = jnp.min(jnp.where(s == mx, blk, nb), axis=0, keepdims=True)
            pick = blk == first
            chosen = jnp.where(pick, jnp.where(r < n_valid, 1.0, 0.0), chosen)
            s = jnp.where(pick, -jnp.inf, s)
        sel_ref[hh] = chosen
        so = jnp.where(key <= qry, _dot(k_own, qth[hh]), NEG_INF)
        s_ref[hh, own_slot, 0:bl, :] = so
        m_own.append(jnp.max(so, axis=0, keepdims=True))

    def scores(j2, ms):
        j0 = pl.multiple_of(j2 * 2 * bl, 2 * bl)
        kj = k_ref[pl.ds(j0, 2 * bl), :]
        out = []
        for hh in range(2):
            s = _dot(kj, qth[hh])
            m = ms[hh]
            for t in range(2):
                picked = sel_ref[hh, pl.ds(2 * j2 + t, 1), :] > 0.0
                st = jnp.where(picked, s[t * bl:(t + 1) * bl], NEG_INF)
                s_ref[hh, j2, t * bl:(t + 1) * bl, :] = st
                m = jnp.maximum(m, jnp.max(st, axis=0, keepdims=True))
            out.append(m)
        return tuple(out)

    ms = lax.fori_loop(0, n_pairs, scores, tuple(m_own))

    l0, acc0 = [], []
    for hh in range(2):
        p = jnp.exp2(s_ref[hh, own_slot, 0:bl, :] - ms[hh])
        l0.append(jnp.sum(p, axis=0, keepdims=True))
        acc0.append(_dot(vt_own, p.astype(bf16)))

    def values(j2, carry):
        vtj = vt_ref[j2]
        ls, accs = carry
        new_l, new_acc = [], []
        for hh in range(2):
            p = jnp.exp2(s_ref[hh, j2] - ms[hh])
            new_l.append(ls[hh] + jnp.sum(p, axis=0, keepdims=True))
            new_acc.append(accs[hh] + _dot(vtj, p.astype(bf16)))
        return tuple(new_l), tuple(new_acc)

    ls, accs = lax.fori_loop(0, n_pairs, values, (tuple(l0), tuple(acc0)))
    out_t = jnp.where(head_lo, accs[0] / ls[0], accs[1] / ls[1])
    o_ref[...] = jnp.transpose(out_t).astype(o_ref.dtype)


def _attn_prompt(qt, k, vt, kmean, bsz, t_len):
    nb = t_len // MOBA_BLOCK
    bl = MOBA_BLOCK
    pair = 2 * DH_A
    tile = qt.shape[2]
    assert tile == 2 * bl and t_len % tile == 0
    n_tiles = t_len // tile
    slots = n_tiles + 1
    return pl.pallas_call(
        _attn_prompt_kernel,
        grid=(bsz, W_A // pair, nb),
        in_specs=[pl.BlockSpec((1, pair, bl), lambda b, hp, qi: (b * n_tiles + qi // 2, hp, qi % 2)),
                  pl.BlockSpec((t_len, pair), lambda b, hp, qi: (b, hp)),
                  pl.BlockSpec((n_tiles, pair, tile), lambda b, hp, qi: (b, hp, 0)),
                  pl.BlockSpec((1, nb, pair), lambda b, hp, qi: (b, 0, hp))],
        out_specs=pl.BlockSpec((bl, pair), lambda b, hp, qi: (b * nb + qi, hp)),
        out_shape=jax.ShapeDtypeStruct((bsz * t_len, W_A), bf16),
        scratch_shapes=[
            pltpu.VMEM((2, nb, bl), f32),
            pltpu.VMEM((2, slots, tile, bl), f32),
        ],
        compiler_params=_cparams(("arbitrary", "arbitrary", "arbitrary"), 48),
        name="attn_prompt",
    )(qt, k, vt, kmean)


def _attn_select_kernel(layer, pt_ref, q_ref, cache_ref, idx_ref, kbuf_ref, sem_ref):
    b = pl.program_id(0)
    n_pages, page = kbuf_ref.shape[0], kbuf_ref.shape[3]
    per_block = MOBA_BLOCK // page
    nbk = n_pages // per_block

    n_waves = sem_ref.shape[0]
    wave = n_pages // n_waves

    def page_copy(p):
        return pltpu.make_async_copy(cache_ref.at[layer, pt_ref[b, p]], kbuf_ref.at[p], sem_ref.at[p // wave])

    for p in range(n_pages):
        page_copy(p).start()

    qb = jnp.broadcast_to(q_ref[0], (H_A, DH_A, page))
    scores = []
    for j in range(nbk):
        if (j * per_block) % wave == 0:
            for p in range(j * per_block, j * per_block + wave):
                page_copy(p).wait()
        tok = jnp.sum(kbuf_ref[j * per_block] * qb, axis=1)
        for t in range(1, per_block):
            tok = tok + jnp.sum(kbuf_ref[j * per_block + t] * qb, axis=1)
        scores.append(jnp.sum(tok, axis=1, keepdims=True) * (1.0 / MOBA_BLOCK))

    idx_ref[0] = jnp.zeros(idx_ref.shape[1:], jnp.int32)
    for r in range(MOBA_TOPK):
        mx = functools.reduce(jnp.maximum, scores)
        first = jnp.full(mx.shape, nbk, jnp.int32)
        for j in reversed(range(nbk)):
            first = jnp.where(scores[j] == mx, j, first)
        idx_ref[0, :, r:r + 1] = first
        scores = [jnp.where(first == j, -jnp.inf, s) for j, s in enumerate(scores)]


SELECT_WAVES = 4


def _attn_select(page_table, q_col, cache_kt, layer):
    bsz, n_pages = page_table.shape
    page = cache_kt.shape[4]
    assert n_pages % SELECT_WAVES == 0 and (n_pages // SELECT_WAVES) % (MOBA_BLOCK // page) == 0
    grid_spec = pltpu.PrefetchScalarGridSpec(
        num_scalar_prefetch=1,
        grid=(bsz,),
        in_specs=[
            pl.BlockSpec((1, H_A, DH_A, 1), lambda b, pt: (b, 0, 0, 0)),
            pl.BlockSpec(memory_space=pl.ANY),
        ],
        out_specs=pl.BlockSpec((1, H_A, LANES), lambda b, pt: (b, 0, 0)),
        scratch_shapes=[pltpu.VMEM((n_pages, H_A, DH_A, page), f32),
                        pltpu.SemaphoreType.DMA((SELECT_WAVES,))],
    )
    return pl.pallas_call(
        functools.partial(_attn_select_kernel, layer),
        grid_spec=grid_spec,
        out_shape=jax.ShapeDtypeStruct((bsz, H_A, LANES), jnp.int32),
        compiler_params=_cparams(("arbitrary",), 40),
        name="attn_select",
    )(page_table, q_col, cache_kt)


def _attn_sample_kernel(layer, pt_ref, sel_ref, q_ref, kn_ref, vn_ref, ck_ref, cv_ref, o_ref,
                        kbuf_ref, vbuf_ref, sem_ref):
    b = pl.program_id(0)
    page = ck_ref.shape[4]
    per_block = MOBA_BLOCK // page

    def copies(h, r, t):
        pg = pt_ref[b, sel_ref[b, h * MOBA_TOPK + r] * per_block + t]
        dst = pl.ds((r * per_block + t) * page, page)
        return (pltpu.make_async_copy(ck_ref.at[layer, pg, h], kbuf_ref.at[h, :, dst], sem_ref.at[0]),
                pltpu.make_async_copy(cv_ref.at[layer, pg, h], vbuf_ref.at[h, :, dst], sem_ref.at[1]))

    slots = [(h, r, t) for h in range(H_A) for r in range(MOBA_TOPK) for t in range(per_block)]
    for s in slots:
        ck, cv = copies(*s)
        ck.start()
        cv.start()
    for s in slots:
        ck, cv = copies(*s)
        ck.wait()
        cv.wait()

    for h in range(H_A):
        qh = q_ref[0, h:h + 1, :]
        kh = kbuf_ref[h].astype(bf16)
        vh = vbuf_ref[h].astype(bf16)
        q8 = jnp.broadcast_to(qh, (SUBLANES, DH_A)).astype(bf16)
        s = _dot(q8, kh)
        s_own = jnp.sum(qh * kn_ref[0, h:h + 1, :], axis=1, keepdims=True)
        m = jnp.maximum(jnp.max(s, axis=1, keepdims=True), s_own)
        p = jnp.exp2(s - m)
        p_own = jnp.exp2(s_own - m)
        den = jnp.sum(p, axis=1, keepdims=True) + p_own
        out = (_dot_nt(p.astype(bf16), vh) + p_own * vn_ref[0, h:h + 1, :]) / den
        o_ref[0, h:h + 1, :] = out[0:1, :].astype(o_ref.dtype)


def _attn_sample(page_table, sel, q, k_new, v_new, cache_k, cache_v, layer):
    bsz = page_table.shape[0]
    n_sel = MOBA_TOPK * MOBA_BLOCK
    tok = pl.BlockSpec((1, H_A, DH_A), lambda b, pt, s: (b, 0, 0))
    grid_spec = pltpu.PrefetchScalarGridSpec(
        num_scalar_prefetch=2,
        grid=(bsz,),
        in_specs=[tok, tok, tok, pl.BlockSpec(memory_space=pl.ANY), pl.BlockSpec(memory_space=pl.ANY)],
        out_specs=tok,
        scratch_shapes=[pltpu.VMEM((H_A, DH_A, n_sel), f32), pltpu.VMEM((H_A, DH_A, n_sel), f32),
                        pltpu.SemaphoreType.DMA((2,))],
    )
    return pl.pallas_call(
        functools.partial(_attn_sample_kernel, layer),
        grid_spec=grid_spec,
        out_shape=jax.ShapeDtypeStruct((bsz, H_A, DH_A), bf16),
        compiler_params=_cparams(("arbitrary",), 32),
        name="attn_sample",
    )(page_table, sel, q, k_new, v_new, cache_k, cache_v)


def _gla_tables():
    c = GLA_CHUNK
    tri = np.tril(np.ones((c, c), np.float32))
    mats = [tri]
    masks = []
    t = np.arange(c)
    for lev in range(GLA_LEVELS):
        h = 1 << lev
        mid = (t // (2 * h)) * (2 * h) + h
        mats.append(tri[mid - 1])
        same = (t[:, None] // (2 * h)) == (t[None, :] // (2 * h))
        masks.append(same & ((t[:, None] & h) != 0) & ((t[None, :] & h) == 0))
    masks.append(np.eye(c, dtype=bool))
    sel = np.concatenate(mats, axis=0)
    lmask = np.stack([np.tile(mk, (H_C, 1)) for mk in masks]).astype(np.float32)
    return sel, lmask


def _gla_prompt_kernel(qk_ref, la_ref, v_ref, r_ref, g_ref, s0_ref, sel_ref, lmask_ref,
                       y_ref, sout_ref, s_ref):
    c = GLA_CHUNK

    @pl.when(pl.program_id(1) == 0)
    def _():
        s_ref[...] = s0_ref[0]

    sel = sel_ref[...]
    g = g_ref[...]
    lane = lax.broadcasted_iota(jnp.int32, (c, QK_C), 1)
    rowc = lax.broadcasted_iota(jnp.int32, (c, QK_C), 0)
    head_of_lane = lane // DK_C
    head_of_row = lax.broadcasted_iota(jnp.int32, (QK_C, DV_C), 0) // DK_C

    def stack_heads(a):
        return jnp.concatenate([jnp.where(head_of_lane == h, a, 0.0) for h in range(H_C)], axis=0)

    def chunk(ci, carry):
        r0 = pl.multiple_of(ci * c, c)
        la = la_ref[pl.ds(r0, c), :]
        la1 = la.astype(bf16)
        rem = la - la1.astype(f32)
        la2 = rem.astype(bf16)
        la3 = (rem - la2.astype(f32)).astype(bf16)
        ball = _dot(sel, la1) + _dot(sel, la2) + _dot(sel, la3)
        b = ball[0:c]
        q = qk_ref[pl.ds(r0, c), 0:QK_C]
        k = qk_ref[pl.ds(r0, c), QK_C:2 * QK_C]
        v = v_ref[pl.ds(r0, c), :].astype(bf16)

        att = lmask_ref[GLA_LEVELS] * _dot_nt(stack_heads(q).astype(bf16), k.astype(bf16))
        for lev in range(GLA_LEVELS):
            ref_b = ball[(lev + 1) * c:(lev + 2) * c]
            right = (rowc & (1 << lev)) != 0
            qt = q * jnp.exp(jnp.where(right, b - ref_b, NEG_INF))
            kt = k * jnp.exp(jnp.where(right, NEG_INF, ref_b - b))
            att = att + lmask_ref[lev] * _dot_nt(stack_heads(qt).astype(bf16), kt.astype(bf16))

        s_old = s_ref[...]
        inter = _dot(stack_heads(q * jnp.exp(b)).astype(bf16), s_old.astype(bf16))
        b_last = b[c - 1:c, :]
        kt_t = jnp.transpose(k * jnp.exp(b_last - b))
        b_t = jnp.transpose(b)
        upd = _dot(kt_t.astype(bf16), v)
        s_new = jnp.exp(b_t[:, c - 1:c]) * s_old
        for h in range(H_C):
            s_new = s_new + jnp.where(head_of_row == h, upd[:, h * DV_C:(h + 1) * DV_C], 0.0)
        s_ref[...] = s_new

        att_bf = att.astype(bf16)
        for h in range(H_C):
            o = (_dot(att_bf[h * c:(h + 1) * c], v[:, h * DV_C:(h + 1) * DV_C])
                 + inter[h * c:(h + 1) * c])
            mu = jnp.mean(o, axis=-1, keepdims=True)
            oc = o - mu
            var = jnp.mean(oc * oc, axis=-1, keepdims=True)
            hs = slice(h * DV_C, (h + 1) * DV_C)
            gate = g[:, hs] * _silu(r_ref[pl.ds(r0, c), hs])
            y_ref[pl.ds(r0, c), hs] = (oc * lax.rsqrt(var + LN_EPS) * gate).astype(y_ref.dtype)
        return carry

    lax.fori_loop(0, qk_ref.shape[0] // c, chunk, 0, unroll=2)
    sout_ref[0] = s_ref[...]


def _gla_prompt(qk, la, v, r, g_gla, s0, bsz, t_len):
    tb = min(GLA_TB, t_len)
    nt = t_len // tb
    sel_np, lmask_np = _gla_tables()
    sel = jnp.asarray(sel_np, bf16)
    lmask = jnp.asarray(lmask_np, f32)
    row = lambda n: pl.BlockSpec((tb, n), lambda b, t: (b * nt + t, 0))
    st = pl.BlockSpec((1, QK_C, DV_C), lambda b, t: (b, 0, 0))
    return pl.pallas_call(
        _gla_prompt_kernel,
        grid=(bsz, nt),
        in_specs=[row(2 * QK_C), row(QK_C), row(V_C), row(V_C),
                  pl.BlockSpec((1, V_C), lambda b, t: (0, 0)), st,
                  pl.BlockSpec(sel.shape, lambda b, t: (0, 0)),
                  pl.BlockSpec(lmask.shape, lambda b, t: (0, 0, 0))],
        out_specs=[row(V_C), st],
        out_shape=[jax.ShapeDtypeStruct((bsz * t_len, V_C), bf16),
                   jax.ShapeDtypeStruct((bsz, QK_C, DV_C), f32)],
        scratch_shapes=[pltpu.VMEM((QK_C, DV_C), f32)],
        compiler_params=_cparams(("arbitrary", "arbitrary"), 40),
        name="gla_prompt",
    )(qk, la, v, r, g_gla, s0, sel, lmask)


def _gla_step_kernel(q_ref, k_ref, la_ref, v_ref, r_ref, g_ref, s_ref, y_ref, sout_ref):
    head_of_row = lax.broadcasted_iota(jnp.int32, (QK_C, DV_C), 0) // DK_C
    v = v_ref[0]
    v_rows = jnp.zeros((QK_C, DV_C), f32)
    for h in range(H_C):
        v_rows = jnp.where(head_of_row == h, v[:, h * DV_C:(h + 1) * DV_C], v_rows)
    s_new = jnp.exp(la_ref[0]) * s_ref[0] + k_ref[0] * v_rows
    sout_ref[0] = s_new
    qs = q_ref[0] * s_new
    g = g_ref[...]
    for h in range(H_C):
        o = jnp.sum(qs[h * DK_C:(h + 1) * DK_C], axis=0, keepdims=True)
        mu = jnp.mean(o, axis=-1, keepdims=True)
        oc = o - mu
        var = jnp.mean(oc * oc, axis=-1, keepdims=True)
        hs = slice(h * DV_C, (h + 1) * DV_C)
        y_ref[0, :, hs] = (oc * lax.rsqrt(var + LN_EPS) * g[:, hs] * _silu(r_ref[0, :, hs])).astype(y_ref.dtype)


def _gla_step(q_col, k_col, la_col, v, r, g_gla, s):
    bsz = s.shape[0]
    col = pl.BlockSpec((1, QK_C, 1), lambda b: (b, 0, 0))
    tok = pl.BlockSpec((1, 1, V_C), lambda b: (b, 0, 0))
    st = pl.BlockSpec((1, QK_C, DV_C), lambda b: (b, 0, 0))
    return pl.pallas_call(
        _gla_step_kernel,
        grid=(bsz,),
        in_specs=[col, col, col, tok, tok, pl.BlockSpec((1, V_C), lambda b: (0, 0)), st],
        out_specs=[tok, st],
        out_shape=[jax.ShapeDtypeStruct((bsz, 1, V_C), bf16), jax.ShapeDtypeStruct(s.shape, f32)],
        compiler_params=_cparams(("arbitrary",), 16),
        name="gla_step",
    )(q_col, k_col, la_col, v, r, g_gla, s)


def _merge_kernel(xb_ref, xf_ref, ya_ref, yb_ref, yc_ref, wg_ref, wa_ref, wb_ref, wc_ref, wo_ref,
                  g_ref, b_ref, hf_ref, hb_ref):
    xb = xb_ref[...]
    merged = None
    for i, (y_ref, w_ref) in enumerate(((ya_ref, wa_ref), (yb_ref, wb_ref), (yc_ref, wc_ref))):
        gate = jax.nn.sigmoid(_dot(xb, wg_ref[:, i * D_MODEL:(i + 1) * D_MODEL]))
        term = gate * _dot(y_ref[...], w_ref[...])
        merged = term if merged is None else merged + term
    mix = _dot(merged.astype(bf16), wo_ref[...])
    h = _layer_norm_rows(ALPHA * xf_ref[...] + mix, g_ref[...], b_ref[...])
    hf_ref[...] = h
    hb_ref[...] = h.astype(bf16)


def _merge(x_bf, x_f, y_a, y_b, y_c, w_gate, w_a, w_b, w_c, w_o, ln_g, ln_b, tm):
    m = x_bf.shape[0]
    row = lambda n: pl.BlockSpec((tm, n), lambda i: (i, 0))
    full = lambda a: pl.BlockSpec(a.shape, lambda i: (0,) * a.ndim)
    return pl.pallas_call(
        _merge_kernel,
        grid=(m // tm,),
        in_specs=[row(D_MODEL), row(D_MODEL), row(W_A), row(W_B), row(V_C),
                  full(w_gate), full(w_a), full(w_b), full(w_c), full(w_o), full(ln_g), full(ln_b)],
        out_specs=[row(D_MODEL), row(D_MODEL)],
        out_shape=[jax.ShapeDtypeStruct((m, D_MODEL), f32), jax.ShapeDtypeStruct((m, D_MODEL), bf16)],
        compiler_params=_cparams(("arbitrary",), 56),
        name="merge",
    )(x_bf, x_f, y_a, y_b, y_c, w_gate, w_a, w_b, w_c, w_o, ln_g, ln_b)


def _group_probs(logits):
    lane = lax.broadcasted_iota(jnp.int32, logits.shape, 1)
    gl = jnp.where(lane < N_GROUPS, logits, -jnp.inf)
    pg = jnp.exp(gl - jnp.max(gl, axis=1, keepdims=True))
    pg = pg / jnp.sum(pg, axis=1, keepdims=True)
    p_top = jnp.max(pg, axis=1, keepdims=True)
    g_idx = jnp.min(jnp.where(pg == p_top, lane, LANES), axis=1, keepdims=True)
    return pg, g_idx


def _combine_weights(logits, pg, g_idx):
    lane = lax.broadcasted_iota(jnp.int32, logits.shape, 1)
    p_top = jnp.sum(jnp.where(lane == g_idx, pg, 0.0), axis=1, keepdims=True)
    e0 = N_GROUPS + g_idx * EXPERTS_PER_GROUP
    el = jnp.where((lane >= e0) & (lane < e0 + EXPERTS_PER_GROUP), logits, -jnp.inf)
    pe = jnp.exp(el - jnp.max(el, axis=1, keepdims=True))
    pe = pe / jnp.sum(pe, axis=1, keepdims=True)
    p1 = jnp.max(pe, axis=1, keepdims=True)
    i1 = jnp.min(jnp.where(pe == p1, lane, LANES), axis=1, keepdims=True)
    rest = jnp.where(lane == i1, -1.0, pe)
    p2 = jnp.max(rest, axis=1, keepdims=True)
    i2 = jnp.min(jnp.where(rest == p2, lane, LANES), axis=1, keepdims=True)
    den = p1 + p2
    return jnp.where(lane == i1, p_top * p1 / den, jnp.where(lane == i2, p_top * p2 / den, 0.0))


def _moe_kernel(hb_ref, hf_ref, wr_ref, br_ref, wg_ref, wu_ref, wd_ref, g_ref, b_ref,
                yf_ref, yb_ref, comb_ref, acc_ref):
    e = pl.program_id(1)
    hb = hb_ref[...]

    @pl.when(e == 0)
    def _():
        logits = _dot(hb, wr_ref[...]) + br_ref[...]
        pg, g_idx = _group_probs(logits)
        comb_ref[...] = _combine_weights(logits, pg, g_idx)
        acc_ref[...] = jnp.zeros_like(acc_ref)

    comb = comb_ref[...]
    lane = lax.broadcasted_iota(jnp.int32, comb.shape, 1)
    cw = jnp.sum(jnp.where(lane == N_GROUPS + e, comb, 0.0), axis=1, keepdims=True)
    act = _silu(_dot(hb, wg_ref[0, 0])) * _dot(hb, wu_ref[0, 0]) * cw
    acc_ref[...] += _dot(act.astype(bf16), wd_ref[0, 0])

    @pl.when(e == N_EXPERTS - 1)
    def _():
        y = _layer_norm_rows(ALPHA * hf_ref[...] + acc_ref[...], g_ref[...], b_ref[...])
        yf_ref[...] = y
        yb_ref[...] = y.astype(bf16)


def _cast_kernel(x_ref, o_ref):
    o_ref[...] = x_ref[...].astype(o_ref.dtype)


def _cast_bf16(w):
    rows, cols = int(np.prod(w.shape[:-1])), w.shape[-1]
    br = (4 * MIB) // (4 * cols)
    assert rows % br == 0
    spec = pl.BlockSpec((br, cols), lambda i: (i, 0))
    out = pl.pallas_call(
        _cast_kernel,
        grid=(rows // br,),
        in_specs=[spec],
        out_specs=spec,
        out_shape=jax.ShapeDtypeStruct((rows, cols), bf16),
        compiler_params=_cparams(("arbitrary",), 32),
        name="cast_bf16",
    )(w.reshape(rows, cols))
    return out.reshape(w.shape)


def _moe(h_bf, h_f, w_router, b_router, w_gate, w_up, w_down, ln_g, ln_b, tm, layer):
    m = h_bf.shape[0]
    row = lambda n: pl.BlockSpec((tm, n), lambda i, e: (i, 0))
    full = lambda a: pl.BlockSpec(a.shape, lambda i, e: (0,) * a.ndim)
    return pl.pallas_call(
        _moe_kernel,
        grid=(m // tm, N_EXPERTS),
        in_specs=[row(D_MODEL), row(D_MODEL), full(w_router), full(b_router),
                  pl.BlockSpec((1, 1, D_MODEL, D_EXPERT), lambda i, e: (layer, e, 0, 0)),
                  pl.BlockSpec((1, 1, D_MODEL, D_EXPERT), lambda i, e: (layer, e, 0, 0)),
                  pl.BlockSpec((1, 1, D_EXPERT, D_MODEL), lambda i, e: (layer, e, 0, 0)),
                  full(ln_g), full(ln_b)],
        out_specs=[row(D_MODEL), row(D_MODEL)],
        out_shape=[jax.ShapeDtypeStruct((m, D_MODEL), f32), jax.ShapeDtypeStruct((m, D_MODEL), bf16)],
        scratch_shapes=[pltpu.VMEM((tm, LANES), f32), pltpu.VMEM((tm, D_MODEL), f32)],
        compiler_params=_cparams(("arbitrary", "arbitrary"), 56),
        name="moe",
    )(h_bf, h_f, w_router, b_router, w_gate, w_up, w_down, ln_g, ln_b)


def _row_tile(m):
    for tm in (512, 256, 128, 64, 32, 16, 8):
        if m % tm == 0:
            return tm
    raise ValueError(f"row count {m} must be a multiple of {SUBLANES}")


def _layer_weights(l, w_in, conv_w, w_fg2, b_fg, g_gla, w_br_a, w_br_b, w_br_c, w_out,
                   ln1_g, ln1_b, ln2_g, ln2_b, w_rg, b_rg, w_re, b_re):
    wi = w_in[l]
    n_front = _P_FL
    gate0 = n_front + FG_RANK
    w_proj = jnp.concatenate(
        [wi[:, :n_front], jnp.pad(wi[:, n_front:gate0], ((0, 0), (0, LANES - FG_RANK)))], axis=1).astype(bf16)
    pad_r = LANES - N_GROUPS - N_EXPERTS
    return dict(
        w_proj=w_proj,
        w_gate=wi[:, gate0:].astype(bf16),
        wfg=jnp.pad(w_fg2[l], ((0, LANES - FG_RANK), (0, 0))).astype(bf16),
        bfg=b_fg[l][None, :],
        conv_w=conv_w[l],
        g_gla=g_gla[l][None, :],
        w_a=w_br_a[l].astype(bf16), w_b=w_br_b[l].astype(bf16), w_c=w_br_c[l].astype(bf16),
        w_o=w_out[l].astype(bf16),
        ln1_g=ln1_g[l][None, :], ln1_b=ln1_b[l][None, :], ln2_g=ln2_g[l][None, :], ln2_b=ln2_b[l][None, :],
        w_router=jnp.pad(jnp.concatenate([w_rg[l], w_re[l]], axis=1), ((0, 0), (0, pad_r))).astype(bf16),
        b_router=jnp.pad(jnp.concatenate([b_rg[l], b_re[l]]), (0, pad_r))[None, :],
    )


MOE_ROWS = 1024


def _finish_layer(p, experts, layer, x_bf, x_f, y_a, y_b, y_c, tm):
    h_f, h_bf = _merge(x_bf, x_f, y_a, y_b, y_c, p["w_gate"], p["w_a"], p["w_b"], p["w_c"], p["w_o"],
                       p["ln1_g"], p["ln1_b"], tm)
    tm_moe = MOE_ROWS if h_f.shape[0] % MOE_ROWS == 0 else tm
    return _moe(h_bf, h_f, p["w_router"], p["b_router"], *experts, p["ln2_g"], p["ln2_b"], tm_moe, layer)


def _prompt_layer(p, experts, layer, x_f, x_bf, bsz, t_len):
    tm = _row_tile(t_len)
    q, k, v, y_b, u_tail, qk, vc, rc, la, k_bf, v_bf, kmean = _proj(
        x_bf, p["w_proj"], p["wfg"], p["bfg"], p["conv_w"], tm, t_len // tm)
    y_a = _attn_prompt(q, k_bf, v_bf, kmean.reshape(bsz, t_len // MOBA_BLOCK, W_A), bsz, t_len)
    y_c, s_fin = _gla_prompt(qk, la, vc, rc, p["g_gla"], jnp.zeros((bsz, QK_C, DV_C), f32), bsz, t_len)
    x_f, x_bf = _finish_layer(p, experts, layer, x_bf, x_f, y_a, y_b, y_c, tm)
    conv_state = u_tail.reshape(bsz, t_len // tm, SUBLANES, W_B)[:, -1, SUBLANES - (CONV_W - 1):, :]
    return (x_f, x_bf, k.reshape(bsz, t_len, H_A, DH_A), v.reshape(bsz, t_len, H_A, DH_A), conv_state,
            s_fin.reshape(bsz, H_C, DK_C, DV_C))


def _sample_layer(p, experts, layer, x_f, x_bf, cache_kt, cache_vt, conv_prev, gla_prev, page_table):
    bsz = x_f.shape[0]
    q, k, v, y_b, u, qk, vc, rc, la = _proj(
        x_bf, p["w_proj"], p["wfg"], p["bfg"], p["conv_w"], bsz, 0, prev=(conv_prev[:, 0], conv_prev[:, 1]))
    heads = lambda a: a.reshape(bsz, H_A, DH_A)
    q_h = heads(q.astype(f32))
    idx = _attn_select(page_table, q_h[..., None], cache_kt, layer)
    sel = idx[:, :, :MOBA_TOPK].reshape(bsz, H_A * MOBA_TOPK)
    y_a = _attn_sample(page_table, sel, q_h, heads(k), heads(v), cache_kt, cache_vt, layer)
    y_c, s_new = _gla_step(qk[:, :QK_C, None], qk[:, QK_C:, None], la[:, :, None], vc[:, None, :],
                           rc[:, None, :], p["g_gla"], gla_prev.reshape(bsz, QK_C, DV_C))
    x_f, x_bf = _finish_layer(p, experts, layer, x_bf, x_f, y_a.reshape(bsz, W_A), y_b, y_c.reshape(bsz, V_C), bsz)
    conv_state = jnp.stack([conv_prev[:, 1], u], axis=1)
    return (x_f, x_bf, k.reshape(bsz, 1, H_A, DH_A), v.reshape(bsz, 1, H_A, DH_A), conv_state,
            s_new.reshape(bsz, H_C, DK_C, DV_C))


def kernel(x_prompt, x_sample, cache_k, cache_v, state_conv, state_gla, page_table, w_in, conv_w, w_fg2, b_fg, g_gla, w_br_a, w_br_b, w_br_c, w_out, ln1_g, ln1_b, ln2_g, ln2_b, w_rg, b_rg, w_re, b_re, w_e_gate, w_e_up, w_e_down):
    bsz, t_len, _ = x_prompt.shape
    dec_b = x_sample.shape[0]
    assert x_sample.shape[1] == 1 and t_len % MOBA_BLOCK == 0 and dec_b % SUBLANES == 0
    assert (page_table.shape[1] * cache_k.shape[2]) % MOBA_BLOCK == 0 and MOBA_BLOCK % cache_k.shape[2] == 0
    cache_kt = jnp.transpose(cache_k, (0, 1, 3, 4, 2))
    cache_vt = jnp.transpose(cache_v, (0, 1, 3, 4, 2))
    xp_f = x_prompt.reshape(bsz * t_len, D_MODEL)
    xs_f = x_sample.reshape(dec_b, D_MODEL)
    xp_bf, xs_bf = xp_f.astype(bf16), xs_f.astype(bf16)
    experts = (_cast_bf16(w_e_gate), _cast_bf16(w_e_up), _cast_bf16(w_e_down))
    outs_p, outs_s = [], []
    for l in range(DEPTH):
        p = _layer_weights(l, w_in, conv_w, w_fg2, b_fg, g_gla, w_br_a, w_br_b, w_br_c, w_out,
                           ln1_g, ln1_b, ln2_g, ln2_b, w_rg, b_rg, w_re, b_re)
        xp_f, xp_bf, *rest_p = _prompt_layer(p, experts, l, xp_f, xp_bf, bsz, t_len)
        xs_f, xs_bf, *rest_s = _sample_layer(p, experts, l, xs_f, xs_bf, cache_kt, cache_vt, state_conv[l],
                                             state_gla[l], page_table)
        outs_p.append(rest_p)
        outs_s.append(rest_s)
    stack = lambda outs, i: jnp.stack([o[i] for o in outs])
    return (xp_f.reshape(bsz, t_len, D_MODEL), xs_f.reshape(dec_b, 1, D_MODEL),
            stack(outs_p, 0), stack(outs_p, 1), stack(outs_p, 2), stack(outs_p, 3).astype(state_gla.dtype),
            stack(outs_s, 0), stack(outs_s, 1), stack(outs_s, 2), stack(outs_s, 3).astype(state_gla.dtype))
```
